```python
import math
import jax
import jax.numpy as jnp
from jax import lax
import numpy as np

D_MODEL = 2048
BATCH = 4
SEQ = 4096
DEPTH = 4

GRID_W = 64
CTX_LEN = 256
N_EVEN = (DEPTH + 1) // 2
N_ODD = DEPTH // 2
EPS = 1e-6

HY_WIDTH = D_MODEL // 2
HY_ORDER = 2
HY_EMB = 33
HY_BANDS = (HY_EMB - 1) // 2
HY_HIDDEN = 64
HY_DECAY_TARGET = 1e-2
HY_FAST_DECAY = 0.3
HY_SLOW_DECAY = 1.5
HY_SPLIT = (HY_ORDER + 1) * HY_WIDTH
GM_WIDTH = D_MODEL // 2
GM_GROUPS = 8
GM_GROUP_DIM = GM_WIDTH // GM_GROUPS
GM_CHUNK = 128
EVEN_IN = HY_SPLIT + 2 * GM_WIDTH
EVEN_OUT = HY_WIDTH + GM_WIDTH

DA_HEADS = 8
DA_HEAD_DIM = 128
DA_WIDTH = DA_HEADS * 2 * DA_HEAD_DIM
ROPE_AXIS_DIM = DA_HEAD_DIM // 2
ROPE_THETA = 10000.0
Q_BLOCK = 128

N_EXPERTS = 32
TOP_K = 4
EXPERT_FF = 768
SWIGLU_ALPHA = 1.702
SWIGLU_LIMIT = 7.0
MOE_BLOCK = 128

kernel_name = 'hybrid_hyena_gmlp_diffattn_moe_dit'


def rms_norm(x, g):
    xf = x.astype(jnp.float32)
    y = xf * lax.rsqrt(jnp.mean(xf * xf, axis=-1, keepdims=True) + EPS)
    return (y * g.astype(jnp.float32)).astype(x.dtype)


def layer_norm(x, g):
    xf = x.astype(jnp.float32)
    xc = xf - jnp.mean(xf, axis=-1, keepdims=True)
    y = xc * lax.rsqrt(jnp.mean(xc * xc, axis=-1, keepdims=True) + EPS)
    return (y * g.astype(jnp.float32)).astype(x.dtype)


def modulate(h, shift, scale):
    return h * (1.0 + scale) + shift


def short_conv3(u, w, b):
    up = jnp.pad(u, ((0, 0), (1, 1), (0, 0)))
    return up[:, :-2] * w[0] + up[:, 1:-1] * w[1] + up[:, 2:] * w[2] + b


def hyena_kernel_spectrum(L, w1, b1, w2, b2, w3, b3, w4):
    f32 = jnp.float32
    t = jnp.linspace(0.0, 1.0, L, dtype=f32)[:, None]
    w = 2.0 * math.pi * jnp.arange(L, dtype=f32)[:, None] / L
    f = jnp.linspace(1e-4, HY_BANDS - 1, HY_BANDS, dtype=f32)[None]
    z = jnp.concatenate([t, jnp.cos(f * w), -jnp.sin(f * w)], axis=-1)
    h = jnp.sin(z @ w1.astype(f32) + b1.astype(f32))
    h = jnp.sin(h @ w2.astype(f32) + b2.astype(f32))
    h = jnp.sin(h @ w3.astype(f32) + b3.astype(f32))
    h = (h @ w4.astype(f32)).reshape(L, HY_ORDER, 2, HY_WIDTH)
    max_decay = math.log(HY_DECAY_TARGET) / HY_FAST_DECAY
    min_decay = math.log(HY_DECAY_TARGET) / HY_SLOW_DECAY
    deltas = jnp.abs(jnp.linspace(min_decay, max_decay, HY_ORDER * HY_WIDTH, dtype=f32)).reshape(HY_ORDER, 1, HY_WIDTH)
    h = h * jnp.exp(-t[:, :, None, None] * deltas)
    fwd, bwd = h[:, :, 0], h[:, :, 1]
    k = jnp.concatenate([fwd, jnp.zeros_like(fwd[:1]), bwd[:0:-1]], axis=0)
    k = k * lax.rsqrt(jnp.sum(k * k, axis=0, keepdims=True) + EPS)
    return jnp.fft.rfft(k, axis=0)


def long_conv(u, k_spec, skip):
    L = u.shape[1]
    uf = u.astype(jnp.float32)
    y = jnp.fft.irfft(jnp.fft.rfft(uf, n=2 * L, axis=1) * k_spec[None], n=2 * L, axis=1)[:, :L]
    return (y + uf * skip.astype(jnp.float32)).astype(u.dtype)


def hyena_mixer(p, conv_w, conv_b, filt, skip):
    L = p.shape[1]
    v, x1, x2 = jnp.split(short_conv3(p, conv_w, conv_b), HY_ORDER + 1, axis=-1)
    k_spec = hyena_kernel_spectrum(L, *filt)
    z = x1 * long_conv(v, k_spec[:, 0], skip[0])
    return x2 * long_conv(z, k_spec[:, 1], skip[1])


def chunk_gmlp(p, ln_g, ws, bs):
    B, L, _ = p.shape
    u, v = jnp.split(p, 2, axis=-1)
    v = layer_norm(v, ln_g).reshape(B, L // GM_CHUNK, GM_CHUNK, GM_GROUPS, GM_GROUP_DIM)
    s = jnp.einsum('gij,bnjgc->bnigc', ws, v) + bs.T[None, None, :, :, None]
    return u * s.reshape(B, L, GM_WIDTH)


def even_mixer(h, w_in, w_out, conv_w, conv_b, filt, skip, ln_g, ws, bs):
    p = h @ w_in
    y_a = hyena_mixer(p[..., :HY_SPLIT], conv_w, conv_b, filt, skip)
    y_b = chunk_gmlp(p[..., HY_SPLIT:], ln_g, ws, bs)
    return jnp.concatenate([y_a, y_b], axis=-1) @ w_out


def axial_rope(x):
    n_tok = x.shape[1]
    rows = n_tok // GRID_W
    row = jnp.repeat(jnp.arange(rows), GRID_W).astype(jnp.float32)
    col = jnp.tile(jnp.arange(GRID_W), rows).astype(jnp.float32)
    inv = ROPE_THETA ** (-jnp.arange(0, ROPE_AXIS_DIM, 2, dtype=jnp.float32) / ROPE_AXIS_DIM)
    ang = jnp.stack([row[:, None] * inv, col[:, None] * inv], axis=1)
    cos = jnp.cos(ang)[None, :, None]
    sin = jnp.sin(ang)[None, :, None]
    xr = x.astype(jnp.float32).reshape(x.shape[:-1] + (2, 2, ROPE_AXIS_DIM // 2))
    x1, x2 = xr[..., 0, :], xr[..., 1, :]
    out = jnp.stack([x1 * cos - x2 * sin, x2 * cos + x1 * sin], axis=-2)
    return out.reshape(x.shape).astype(x.dtype)


def diff_attend(q, k, v, lam):
    s = jnp.einsum('bqhmd,bkhmd->bhmqk', q, k, preferred_element_type=jnp.float32) / math.sqrt(DA_HEAD_DIM)
    p = jax.nn.softmax(s, axis=-1)
    a = p[:, :, 0] - lam * p[:, :, 1]
    return jnp.einsum('bhqk,bkhe->bqhe', a, v.astype(jnp.float32)).astype(v.dtype)


def diff_attention(h_lat, h_ctx, w_qkv, w_out, lam_p, subln_g, lam_init, need_ctx):
    B, L, _ = h_lat.shape
    C = h_ctx.shape[1]
    q_l, k_l, v_l = jnp.split(h_lat @ w_qkv, 3, axis=-1)
    q_c, k_c, v_c = jnp.split(h_ctx @ w_qkv, 3, axis=-1)
    q_l = axial_rope(q_l.reshape(B, L, 2 * DA_HEADS, DA_HEAD_DIM)).reshape(B, L, DA_HEADS, 2, DA_HEAD_DIM)
    k_l = axial_rope(k_l.reshape(B, L, 2 * DA_HEADS, DA_HEAD_DIM)).reshape(B, L, DA_HEADS, 2, DA_HEAD_DIM)
    v_l = v_l.reshape(B, L, DA_HEADS, 2 * DA_HEAD_DIM)
    q_c = q_c.reshape(B, C, DA_HEADS, 2, DA_HEAD_DIM)
    k_c = k_c.reshape(B, C, DA_HEADS, 2, DA_HEAD_DIM)
    v_c = v_c.reshape(B, C, DA_HEADS, 2 * DA_HEAD_DIM)
    lp = lam_p.astype(jnp.float32)
    lam = jnp.exp(jnp.sum(lp[0] * lp[1])) - jnp.exp(jnp.sum(lp[2] * lp[3])) + lam_init
    k_all = jnp.concatenate([k_c, k_l], axis=1)
    v_all = jnp.concatenate([v_c, v_l], axis=1)
    q_blocks = q_l.reshape(B, L // Q_BLOCK, Q_BLOCK, DA_HEADS, 2, DA_HEAD_DIM).swapaxes(0, 1)
    o_l = lax.map(lambda qb: diff_attend(qb, k_all, v_all, lam), q_blocks)
    o_l = o_l.swapaxes(0, 1).reshape(B, L, DA_HEADS, 2 * DA_HEAD_DIM)

    def project_out(o):
        n = o.shape[1]
        return (rms_norm(o, subln_g) * (1.0 - lam_init)).reshape(B, n, DA_WIDTH) @ w_out

    y_l = project_out(o_l)
    y_c = project_out(diff_attend(q_c, k_c, v_c, lam)) if need_ctx else None
    return y_l, y_c


def moe_ffn(tokens, wr, br, w1, b1, w2, b2):
    n, d = tokens.shape
    f32 = jnp.float32

    def block(xb):
        logits = jnp.einsum('nd,de->ne', xb, wr, preferred_element_type=f32) + br.astype(f32)
        top_v, top_i = lax.top_k(logits, TOP_K)
        gates = jax.nn.softmax(top_v, axis=-1)
        comb = jnp.einsum('nk,nke->ne', gates, jax.nn.one_hot(top_i, N_EXPERTS, dtype=f32))
        hh = jnp.einsum('nd,edf->nef', xb, w1, preferred_element_type=f32) + b1.astype(f32)
        glu, lin = jnp.split(hh, 2, axis=-1)
        glu = jnp.minimum(glu, SWIGLU_LIMIT)
        lin = jnp.clip(lin, -SWIGLU_LIMIT, SWIGLU_LIMIT)
        act = glu * jax.nn.sigmoid(SWIGLU_ALPHA * glu) * (lin + 1.0) * comb[:, :, None]
        y = jnp.einsum('nef,efd->nd', act.astype(xb.dtype), w2, preferred_element_type=f32) + comb @ b2.astype(f32)
        return y.astype(xb.dtype)

    out = lax.map(block, tokens.reshape(n // MOE_BLOCK, MOE_BLOCK, d))
    return out.reshape(n, d)


def setup_inputs(seed: int = 0) -> dict:
    key = jax.random.key(seed)
    ks = iter(jax.random.split(key, 40))

    def nrm(shape, scale):
        return jax.random.normal(next(ks), shape, jnp.float32) * scale

    D = D_MODEL
    F = EXPERT_FF
    return {
        'x': nrm((BATCH, SEQ, D), 1.0),
        'c': nrm((BATCH, D), 1.0),
        'ctx': nrm((BATCH, CTX_LEN, D), 1.0),
        'c_ctx': nrm((D,), 1.0),
        'ada_w': nrm((DEPTH, D, 6 * D), D ** -0.5),
        'ada_b': nrm((DEPTH, 6 * D), 0.01),
        'norm_g': 1.0 + nrm((DEPTH, 2, D), 0.01),
        'final_g': 1.0 + nrm((D,), 0.01),
        'ev_w_in': nrm((N_EVEN, D, EVEN_IN), D ** -0.5),
        'ev_w_out': nrm((N_EVEN, EVEN_OUT, D), EVEN_OUT ** -0.5),
        'hy_conv_w': nrm((N_EVEN, 3, HY_SPLIT), 3 ** -0.5),
        'hy_conv_b': nrm((N_EVEN, HY_SPLIT), 0.01),
        'hy_f_w1': nrm((N_EVEN, HY_EMB, HY_HIDDEN), HY_EMB ** -0.5),
        'hy_f_b1': nrm((N_EVEN, HY_HIDDEN), 0.02),
        'hy_f_w2': nrm((N_EVEN, HY_HIDDEN, HY_HIDDEN), HY_HIDDEN ** -0.5),
        'hy_f_b2': nrm((N_EVEN, HY_HIDDEN), 0.02),
        'hy_f_w3': nrm((N_EVEN, HY_HIDDEN, HY_HIDDEN), HY_HIDDEN ** -0.5),
        'hy_f_b3': nrm((N_EVEN, HY_HIDDEN), 0.02),
        'hy_f_w4': nrm((N_EVEN, HY_HIDDEN, HY_ORDER * 2 * HY_WIDTH), HY_HIDDEN ** -0.5),
        'hy_skip': nrm((N_EVEN, HY_ORDER, HY_WIDTH), 0.5),
        'gm_ln_g': 1.0 + nrm((N_EVEN, GM_WIDTH), 0.01),
        'gm_ws': nrm((N_EVEN, GM_GROUPS, GM_CHUNK, GM_CHUNK), GM_CHUNK ** -0.5),
        'gm_bs': 1.0 + nrm((N_EVEN, GM_GROUPS, GM_CHUNK), 0.01),
        'od_w_qkv': nrm((N_ODD, D, 3 * DA_WIDTH), D ** -0.5),
        'od_w_out': nrm((N_ODD, DA_WIDTH, D), DA_WIDTH ** -0.5),
        'od_lambda': nrm((N_ODD, 4, DA_HEAD_DIM), 0.1),
        'od_subln_g': 1.0 + nrm((N_ODD, 2 * DA_HEAD_DIM), 0.01),
        'moe_wr': nrm((DEPTH, D, N_EXPERTS), D ** -0.5),
        'moe_br': nrm((DEPTH, N_EXPERTS), 0.01),
        'moe_w1': nrm((DEPTH, N_EXPERTS, D, 2 * F), D ** -0.5),
        'moe_b1': nrm((DEPTH, N_EXPERTS, 2 * F), 0.01),
        'moe_w2': nrm((DEPTH, N_EXPERTS, F, D), F ** -0.5),
        'moe_b2': nrm((DEPTH, N_EXPERTS, D), 0.01),
    }


def reference(x, c, ctx, c_ctx, ada_w, ada_b, norm_g, final_g, ev_w_in, ev_w_out, hy_conv_w, hy_conv_b,
              hy_f_w1, hy_f_b1, hy_f_w2, hy_f_b2, hy_f_w3, hy_f_b3, hy_f_w4, hy_skip, gm_ln_g, gm_ws, gm_bs,
              od_w_qkv, od_w_out, od_lambda, od_subln_g, moe_wr, moe_br, moe_w1, moe_b1, moe_w2, moe_b2):
    B, L, D = x.shape
    n_ctx = ctx.shape[1]
    s_lat = jax.nn.silu(c)
    s_ctx = jax.nn.silu(c_ctx)
    cx = ctx
    for l in range(DEPTH):
        last = l == DEPTH - 1
        i = l // 2
        sh1, sc1, g1, sh2, sc2, g2 = jnp.split((s_lat @ ada_w[l] + ada_b[l])[:, None, :], 6, axis=-1)
        csh1, csc1, cg1, csh2, csc2, cg2 = jnp.split((s_ctx @ ada_w[l] + ada_b[l])[None, None, :], 6, axis=-1)
        h_lat = modulate(rms_norm(x, norm_g[l, 0]), sh1, sc1)
        h_ctx = modulate(rms_norm(cx, norm_g[l, 0]), csh1, csc1)
        if l % 2 == 0:
            filt = (hy_f_w1[i], hy_f_b1[i], hy_f_w2[i], hy_f_b2[i], hy_f_w3[i], hy_f_b3[i], hy_f_w4[i])
            y_lat = even_mixer(h_lat, ev_w_in[i], ev_w_out[i], hy_conv_w[i], hy_conv_b[i], filt, hy_skip[i],
                               gm_ln_g[i], gm_ws[i], gm_bs[i])
            y_ctx = None if last else even_mixer(h_ctx, ev_w_in[i], ev_w_out[i], hy_conv_w[i], hy_conv_b[i], filt,
                                                 hy_skip[i], gm_ln_g[i], gm_ws[i], gm_bs[i])
        else:
            lam_init = 0.8 - 0.6 * math.exp(-0.3 * l)
            y_lat, y_ctx = diff_attention(h_lat, h_ctx, od_w_qkv[i], od_w_out[i], od_lambda[i], od_subln_g[i],
                                          lam_init, not last)
        x = x + g1 * y_lat
        h_lat = modulate(rms_norm(x, norm_g[l, 1]), sh2, sc2)
        if last:
            out = moe_ffn(h_lat.reshape(B * L, D), moe_wr[l], moe_br[l], moe_w1[l], moe_b1[l], moe_w2[l], moe_b2[l])
            x = x + g2 * out.reshape(B, L, D)
        else:
            cx = cx + cg1 * y_ctx
            h_ctx = modulate(rms_norm(cx, norm_g[l, 1]), csh2, csc2)
            tok = jnp.concatenate([h_lat.reshape(B * L, D), h_ctx.reshape(B * n_ctx, D)], axis=0)
            out = moe_ffn(tok, moe_wr[l], moe_br[l], moe_w1[l], moe_b1[l], moe_w2[l], moe_b2[l])
            x = x + g2 * out[:B * L].reshape(B, L, D)
            cx = cx + cg2 * out[B * L:].reshape(B, n_ctx, D)
    return rms_norm(x, final_g)
```

```python
import functools
import math

import jax
import jax.numpy as jnp
from jax import lax
from jax.experimental import pallas as pl
from jax.experimental.pallas import tpu as pltpu

F32 = jnp.float32
BF16 = jnp.bfloat16

EPS = 1e-6
GRID_W = 64

HY_ORDER = 2
HY_EMB = 33
HY_BANDS = (HY_EMB - 1) // 2
HY_DECAY_TARGET = 1e-2
HY_FAST_DECAY = 0.3
HY_SLOW_DECAY = 1.5

GM_GROUPS = 8
GM_CHUNK = 128

DA_HEADS = 8
DA_HEAD_DIM = 128
ROPE_AXIS_DIM = DA_HEAD_DIM // 2
ROPE_THETA = 10000.0

TOP_K = 4
SWIGLU_ALPHA = 1.702
SWIGLU_LIMIT = 7.0

MIB = 1024 * 1024
VMEM_LIMIT_BYTES = 56 * MIB


def _params(n_grid_dims, vmem=VMEM_LIMIT_BYTES):
    return pltpu.CompilerParams(dimension_semantics=("arbitrary",) * n_grid_dims, vmem_limit_bytes=vmem)


def _tile(n, want):
    t = min(n, want)
    assert n % t == 0, (n, want)
    return t


def _mm_kernel(a_ref, w_ref, o_ref):
    o_ref[...] = jnp.dot(a_ref[...].astype(BF16), w_ref[...].astype(BF16),
                         preferred_element_type=F32).astype(o_ref.dtype)


def matmul(a, w, layer, *, tm=512, tn=512):
    M, K = a.shape
    N = w.shape[2]
    tm, tn = _tile(M, tm), _tile(N, tn)
    return pl.pallas_call(
        _mm_kernel,
        grid=(N // tn, M // tm),
        in_specs=[pl.BlockSpec((tm, K), lambda j, i: (i, 0)),
                  pl.BlockSpec((None, K, tn), lambda j, i: (layer, 0, j))],
        out_specs=pl.BlockSpec((tm, tn), lambda j, i: (i, j)),
        out_shape=jax.ShapeDtypeStruct((M, N), F32),
        compiler_params=_params(2),
    )(a, w)


def _lmm_kernel(m_ref, u_ref, o_ref):
    o_ref[...] = jnp.dot(m_ref[...], u_ref[...].astype(BF16), preferred_element_type=F32)


def left_matmul(mat, u, *, tm=512, tn=512):
    R, K = mat.shape
    B, _, C = u.shape
    tm, tn = _tile(R, tm), _tile(C, tn)
    return pl.pallas_call(
        _lmm_kernel,
        grid=(B, C // tn, R // tm),
        in_specs=[pl.BlockSpec((tm, K), lambda b, j, i: (i, 0)),
                  pl.BlockSpec((None, K, tn), lambda b, j, i: (b, 0, j))],
        out_specs=pl.BlockSpec((None, tm, tn), lambda b, j, i: (b, i, j)),
        out_shape=jax.ShapeDtypeStruct((B, R, C), F32),
        compiler_params=_params(3),
    )(mat, u)


def dft_matrices(L):
    n = 2 * L
    r = lax.broadcasted_iota(jnp.int32, (n, L), 0)
    t = lax.broadcasted_iota(jnp.int32, (n, L), 1)
    is_cos = r <= L
    f = jnp.where(is_cos, r, r - L)
    ang = ((f * t) % n).astype(F32) * (2.0 * math.pi / n)
    fwd = jnp.where(is_cos, jnp.cos(ang), -jnp.sin(ang))
    weight = jnp.where((r == 0) | (r == L), 1.0 / n, 2.0 / n)
    return fwd.astype(BF16), (fwd * weight).T.astype(BF16)


def hyena_filters(L, w1, b1, w2, b2, w3, b3, w4, width):
    hp = lax.Precision.HIGHEST
    t = jnp.linspace(0.0, 1.0, L, dtype=F32)[:, None]
    w = 2.0 * math.pi * jnp.arange(L, dtype=F32)[:, None] / L
    f = jnp.linspace(1e-4, HY_BANDS - 1, HY_BANDS, dtype=F32)[None]
    z = jnp.concatenate([t, jnp.cos(f * w), -jnp.sin(f * w)], axis=-1)
    h = jnp.sin(jnp.dot(z, w1, precision=hp) + b1)
    h = jnp.sin(jnp.dot(h, w2, precision=hp) + b2)
    h = jnp.sin(jnp.dot(h, w3, precision=hp) + b3)
    h = jnp.dot(h, w4, precision=hp).reshape(L, HY_ORDER, 2, width)
    max_decay = math.log(HY_DECAY_TARGET) / HY_FAST_DECAY
    min_decay = math.log(HY_DECAY_TARGET) / HY_SLOW_DECAY
    deltas = jnp.abs(jnp.linspace(min_decay, max_decay, HY_ORDER * width, dtype=F32)).reshape(HY_ORDER, 1, width)
    h = h * jnp.exp(-t[:, :, None, None] * deltas)
    fwd, bwd = h[:, :, 0], h[:, :, 1]
    bwd = bwd.at[0].set(0.0)
    scale = lax.rsqrt(jnp.sum(fwd * fwd, axis=0) + jnp.sum(bwd * bwd, axis=0) + EPS)
    return (fwd * scale).reshape(L, -1), (bwd * scale).reshape(L, -1)


def filter_spectrum(L, filt, width, dft_fwd):
    fwd, bwd = hyena_filters(L, *filt, width)
    spec = left_matmul(dft_fwd, jnp.stack([fwd, bwd]))
    r = lax.broadcasted_iota(jnp.int32, (2 * L, 1), 0)
    return spec[0] + jnp.where(r <= L, 1.0, -1.0) * spec[1]


def long_conv(u, k_spec, skip, dft_fwd, dft_inv):
    L = u.shape[1]
    z = left_matmul(dft_fwd, u)
    za, zb = z[:, :L], z[:, L:]
    ka, kb = k_spec[None, :L], k_spec[None, L:]
    first = (lax.broadcasted_iota(jnp.int32, (1, L, 1), 1) == 0)
    bb = zb * kb
    ya = za * ka - jnp.where(first, 0.0, bb)
    yb = jnp.where(first, bb, za * kb + zb * ka)
    y = left_matmul(dft_inv, jnp.concatenate([ya, yb], axis=1))
    return y + u * skip


def short_conv3(u, w, b):
    up = jnp.pad(u, ((0, 0), (1, 1), (0, 0)))
    return up[:, :-2] * w[0] + up[:, 1:-1] * w[1] + up[:, 2:] * w[2] + b


def hyena_mixer(p, conv_w, conv_b, filt, skip):
    L = p.shape[1]
    width = p.shape[2] // (HY_ORDER + 1)
    v, x1, x2 = jnp.split(short_conv3(p, conv_w, conv_b), HY_ORDER + 1, axis=-1)
    dft_fwd, dft_inv = dft_matrices(L)
    k_spec = filter_spectrum(L, filt, width, dft_fwd)
    z = x1 * long_conv(v, k_spec[:, :width], skip[0], dft_fwd, dft_inv)
    return x2 * long_conv(z, k_spec[:, width:], skip[1], dft_fwd, dft_inv)


def _gmlp_kernel(u_ref, v_ref, g_ref, ws_ref, bs_ref, o_ref):
    v = v_ref[...]
    vc = v - jnp.mean(v, axis=-1, keepdims=True)
    vn = vc * lax.rsqrt(jnp.mean(vc * vc, axis=-1, keepdims=True) + EPS) * g_ref[...]
    vn = vn.astype(BF16)
    gd = v.shape[1] // GM_GROUPS
    for g in range(GM_GROUPS):
        cols = slice(g * gd, (g + 1) * gd)
        s = jnp.dot(ws_ref[g].astype(BF16), vn[:, cols], preferred_element_type=F32)
        o_ref[:, cols] = u_ref[:, cols] * (s + bs_ref[:, cols])


def chunk_gmlp(p, col0, width, ln_g, ws, bs, layer):
    B, T, _ = p.shape
    assert col0 % width == 0 and T % GM_CHUNK == 0
    cb = col0 // width
    bs_full = jnp.repeat(bs[layer].T, width // GM_GROUPS, axis=1)
    return pl.pallas_call(
        _gmlp_kernel,
        grid=(B, T // GM_CHUNK),
        in_specs=[pl.BlockSpec((None, GM_CHUNK, width), lambda b, n: (b, n, cb)),
                  pl.BlockSpec((None, GM_CHUNK, width), lambda b, n: (b, n, cb + 1)),
                  pl.BlockSpec((None, 1, width), lambda b, n: (layer, 0, 0)),
                  pl.BlockSpec((None, GM_GROUPS, GM_CHUNK, GM_CHUNK), lambda b, n: (layer, 0, 0, 0)),
                  pl.BlockSpec((GM_CHUNK, width), lambda b, n: (0, 0))],
        out_specs=pl.BlockSpec((None, GM_CHUNK, width), lambda b, n: (b, n, 0)),
        out_shape=jax.ShapeDtypeStruct((B, T, width), F32),
        compiler_params=_params(2),
    )(p, p, ln_g.reshape(ln_g.shape[0], 1, width), ws, bs_full)


def _attn_kernel(lam_ref, q_ref, k_ref, v_ref, g_ref, o_ref, *, out_scale):
    lam = lam_ref[0]
    q = q_ref[...]
    k = k_ref[...]
    d = q.shape[1] // 2
    maps = []
    for m in range(2):
        cols = slice(m * d, (m + 1) * d)
        s = lax.dot_general(q[:, cols], k[:, cols], (((1,), (1,)), ((), ())), preferred_element_type=F32)
        e = jnp.exp(s - jnp.max(s, axis=-1, keepdims=True))
        maps.append((e, jnp.sum(e, axis=-1, keepdims=True)))
    (e0, l0), (e1, l1) = maps
    a = e0 * (1.0 / l0) - e1 * (lam / l1)
    o = jnp.dot(a.astype(BF16), v_ref[...], preferred_element_type=F32)
    o = o * lax.rsqrt(jnp.mean(o * o, axis=-1, keepdims=True) + EPS)
    o_ref[...] = (o * (g_ref[...] * out_scale)).astype(o_ref.dtype)


def diff_attention(q, k, v, lam, subln_g, layer, out_scale, *, q_row0, n_q, n_k, tq=256):
    B, T, W = q.shape
    hw = 2 * DA_HEAD_DIM
    tq = _tile(n_q, tq)
    assert q_row0 % tq == 0 and W == DA_HEADS * hw
    qb0 = q_row0 // tq
    kernel = functools.partial(_attn_kernel, out_scale=out_scale)
    return pl.pallas_call(
        kernel,
        grid=(B, DA_HEADS, n_q // tq),
        in_specs=[pl.BlockSpec(memory_space=pltpu.SMEM),
                  pl.BlockSpec((None, tq, hw), lambda b, h, i: (b, qb0 + i, h)),
                  pl.BlockSpec((None, n_k, hw), lambda b, h, i: (b, 0, h)),
                  pl.BlockSpec((None, n_k, hw), lambda b, h, i: (b, 0, h)),
                  pl.BlockSpec((None, 1, hw), lambda b, h, i: (layer, 0, 0))],
        out_specs=pl.BlockSpec((None, tq, hw), lambda b, h, i: (b, i, h)),
        out_shape=jax.ShapeDtypeStruct((B, n_q, W), BF16),
        compiler_params=_params(3),
    )(lam, q, k, v, subln_g.reshape(subln_g.shape[0], 1, hw))


def axial_rope(x):
    n_tok = x.shape[1]
    rows = n_tok // GRID_W
    row = jnp.repeat(jnp.arange(rows), GRID_W).astype(F32)
    col = jnp.tile(jnp.arange(GRID_W), rows).astype(F32)
    inv = ROPE_THETA ** (-jnp.arange(0, ROPE_AXIS_DIM, 2, dtype=F32) / ROPE_AXIS_DIM)
    ang = jnp.stack([row[:, None] * inv, col[:, None] * inv], axis=1)
    cos = jnp.cos(ang)[None, :, None]
    sin = jnp.sin(ang)[None, :, None]
    xr = x.reshape(x.shape[:-1] + (2, 2, ROPE_AXIS_DIM // 2))
    x1, x2 = xr[..., 0, :], xr[..., 1, :]
    out = jnp.stack([x1 * cos - x2 * sin, x2 * cos + x1 * sin], axis=-2)
    return out.reshape(x.shape)


def _router_kernel(x_ref, w_ref, b_ref, o_ref):
    o_ref[...] = jnp.dot(x_ref[...], w_ref[...], precision=lax.Precision.HIGHEST,
                         preferred_element_type=F32) + b_ref[...]


def router_logits(x, wr, br, layer, *, tm=512):
    M, K = x.shape
    E = wr.shape[2]
    tm = _tile(M, tm)
    return pl.pallas_call(
        _router_kernel,
        grid=(M // tm,),
        in_specs=[pl.BlockSpec((tm, K), lambda i: (i, 0)),
                  pl.BlockSpec((None, K, E), lambda i: (layer, 0, 0)),
                  pl.BlockSpec((None, 1, E), lambda i: (layer, 0, 0))],
        out_specs=pl.BlockSpec((tm, E), lambda i: (i, 0)),
        out_shape=jax.ShapeDtypeStruct((M, E), F32),
        compiler_params=_params(1),
    )(x, wr, br.reshape(br.shape[0], 1, E))


def _moe_kernel(te_ref, nt_ref, x_ref, g_ref, w1_ref, b1_ref, w2_ref, b2_ref, o_ref, *, n_chunks):
    t = pl.program_id(0)

    @pl.when(t < nt_ref[0])
    def _():
        x = x_ref[...]
        ff = w2_ref.shape[0]
        fc = ff // n_chunks
        y = jnp.zeros(o_ref.shape, F32)
        for c in range(n_chunks):
            glu_cols = slice(c * fc, (c + 1) * fc)
            lin_cols = slice(ff + c * fc, ff + (c + 1) * fc)
            glu = jnp.dot(x, w1_ref[:, glu_cols].astype(BF16), preferred_element_type=F32) + b1_ref[:, glu_cols]
            lin = jnp.dot(x, w1_ref[:, lin_cols].astype(BF16), preferred_element_type=F32) + b1_ref[:, lin_cols]
            glu = jnp.minimum(glu, SWIGLU_LIMIT)
            lin = jnp.clip(lin, -SWIGLU_LIMIT, SWIGLU_LIMIT)
            act = glu * jax.nn.sigmoid(SWIGLU_ALPHA * glu) * (lin + 1.0)
            y = y + jnp.dot(act.astype(BF16), w2_ref[glu_cols, :].astype(BF16), preferred_element_type=F32)
        o_ref[...] = (y + b2_ref[...]) * g_ref[...]

    @pl.when(t >= nt_ref[0])
    def _():
        o_ref[...] = jnp.zeros(o_ref.shape, F32)


def moe_experts(xs, row_gate, tile_expert, n_tiles_used, w1, b1, w2, b2, layer, *, tm):
    P, D = xs.shape
    E, _, F2 = w1.shape[1:]
    ff = F2 // 2
    n_tiles = P // tm
    kernel = functools.partial(_moe_kernel, n_chunks=3)
    grid_spec = pltpu.PrefetchScalarGridSpec(
        num_scalar_prefetch=2,
        grid=(n_tiles,),
        in_specs=[pl.BlockSpec((tm, D), lambda t, te, nt: (t, 0)),
                  pl.BlockSpec((tm, 1), lambda t, te, nt: (t, 0)),
                  pl.BlockSpec((None, None, D, F2), lambda t, te, nt: (layer, te[t], 0, 0)),
                  pl.BlockSpec((None, None, 1, F2), lambda t, te, nt: (layer, te[t], 0, 0)),
                  pl.BlockSpec((None, None, ff, D), lambda t, te, nt: (layer, te[t], 0, 0)),
                  pl.BlockSpec((None, None, 1, D), lambda t, te, nt: (layer, te[t], 0, 0))],
        out_specs=pl.BlockSpec((tm, D), lambda t, te, nt: (t, 0)),
    )
    return pl.pallas_call(
        kernel,
        grid_spec=grid_spec,
        out_shape=jax.ShapeDtypeStruct((P, D), F32),
        compiler_params=_params(1),
    )(tile_expert, n_tiles_used, xs, row_gate, w1, b1.reshape(b1.shape[0], E, 1, F2), w2,
      b2.reshape(b2.shape[0], E, 1, D))


def moe_ffn(tokens, wr, br, w1, b1, w2, b2, layer, *, tm=256):
    N, D = tokens.shape
    E = wr.shape[2]
    logits = router_logits(tokens, wr, br, layer)
    top_v, top_i = lax.top_k(logits, TOP_K)
    gates = jax.nn.softmax(top_v, axis=-1)

    P = N * TOP_K
    n_tiles = P // tm + E
    pair_e = top_i.reshape(P)
    order = jnp.argsort(pair_e, stable=True)
    sorted_e = pair_e[order]
    counts = jnp.zeros((E,), jnp.int32).at[pair_e].add(1)
    tiles_per = (counts + tm - 1) // tm
    tile_end = jnp.cumsum(tiles_per)
    start_padded = (tile_end - tiles_per) * tm
    start_sorted = jnp.cumsum(counts) - counts
    dest = start_padded[sorted_e] + (jnp.arange(P, dtype=jnp.int32) - start_sorted[sorted_e])
    row_token = jnp.zeros((n_tiles * tm,), jnp.int32).at[dest].set((order // TOP_K).astype(jnp.int32))
    row_gate = jnp.zeros((n_tiles * tm,), F32).at[dest].set(gates.reshape(P)[order])
    pair_row = jnp.zeros((P,), jnp.int32).at[order].set(dest)
    n_used = tile_end[-1:].astype(jnp.int32)
    tile_ids = jnp.minimum(jnp.arange(n_tiles, dtype=jnp.int32), n_used - 1)
    tile_expert = jnp.searchsorted(tile_end, tile_ids, side="right").astype(jnp.int32)

    xs = tokens.astype(BF16)[row_token]
    ys = moe_experts(xs, row_gate[:, None], tile_expert, n_used, w1, b1, w2, b2, layer, tm=tm)
    return ys[pair_row].reshape(N, TOP_K, D).sum(axis=1)


def rms_norm(x, g):
    return x * lax.rsqrt(jnp.mean(x * x, axis=-1, keepdims=True) + EPS) * g


def kernel(x, c, ctx, c_ctx, ada_w, ada_b, norm_g, final_g, ev_w_in, ev_w_out, hy_conv_w, hy_conv_b,
           hy_f_w1, hy_f_b1, hy_f_w2, hy_f_b2, hy_f_w3, hy_f_b3, hy_f_w4, hy_skip, gm_ln_g, gm_ws, gm_bs,
           od_w_qkv, od_w_out, od_lambda, od_subln_g, moe_wr, moe_br, moe_w1, moe_b1, moe_w2, moe_b2):
    B, L, D = x.shape
    n_ctx = ctx.shape[1]
    T = n_ctx + L
    depth = ada_w.shape[0]
    hy_width = hy_skip.shape[2]
    hy_split = (HY_ORDER + 1) * hy_width
    gm_width = gm_ln_g.shape[1]

    xs = jnp.concatenate([ctx, x], axis=1)
    is_ctx = (jnp.arange(T) < n_ctx)[None, :, None]
    cond = jnp.concatenate([jax.nn.silu(c), jax.nn.silu(c_ctx)[None], jnp.zeros((-(B + 1) % 8, D), F32)], axis=0)

    for l in range(depth):
        i = l // 2
        ada = matmul(cond.astype(BF16), ada_w, l, tm=cond.shape[0], tn=1024) + ada_b[l]
        mods = [jnp.where(is_ctx, m[B][None, None, :], m[:B][:, None, :]) for m in jnp.split(ada, 6, axis=-1)]
        sh1, sc1, g1, sh2, sc2, g2 = mods

        h = (rms_norm(xs, norm_g[l, 0]) * (1.0 + sc1) + sh1).astype(BF16).reshape(B * T, D)
        if l % 2 == 0:
            p = matmul(h, ev_w_in, i).reshape(B, T, -1)
            filt = (hy_f_w1[i], hy_f_b1[i], hy_f_w2[i], hy_f_b2[i], hy_f_w3[i], hy_f_b3[i], hy_f_w4[i])
            y_a = jnp.concatenate(
                [hyena_mixer(p[:, :n_ctx, :hy_split], hy_conv_w[i], hy_conv_b[i], filt, hy_skip[i]),
                 hyena_mixer(p[:, n_ctx:, :hy_split], hy_conv_w[i], hy_conv_b[i], filt, hy_skip[i])], axis=1)
            y_b = chunk_gmlp(p, hy_split, gm_width, gm_ln_g, gm_ws, gm_bs, i)
            y_in = jnp.concatenate([y_a, y_b], axis=-1).astype(BF16).reshape(B * T, -1)
            y = matmul(y_in, ev_w_out, i)
        else:
            lam_init = 0.8 - 0.6 * math.exp(-0.3 * l)
            qkv = matmul(h, od_w_qkv, i).reshape(B, T, 3, 2 * DA_HEADS, DA_HEAD_DIM)
            q, k, v = qkv[:, :, 0], qkv[:, :, 1], qkv[:, :, 2]
            q = jnp.concatenate([q[:, :n_ctx], axial_rope(q[:, n_ctx:])], axis=1) * (1.0 / math.sqrt(DA_HEAD_DIM))
            k = jnp.concatenate([k[:, :n_ctx], axial_rope(k[:, n_ctx:])], axis=1)
            q, k, v = (a.astype(BF16).reshape(B, T, -1) for a in (q, k, v))
            lp = od_lambda[i]
            lam = (jnp.exp(jnp.sum(lp[0] * lp[1])) - jnp.exp(jnp.sum(lp[2] * lp[3])) + lam_init).reshape(1)
            o_ctx = diff_attention(q, k, v, lam, od_subln_g, i, 1.0 - lam_init, q_row0=0, n_q=n_ctx, n_k=n_ctx)
            o_lat = diff_attention(q, k, v, lam, od_subln_g, i, 1.0 - lam_init, q_row0=n_ctx, n_q=L, n_k=T)
            y = matmul(jnp.concatenate([o_ctx, o_lat], axis=1).reshape(B * T, -1), od_w_out, i)
        xs = xs + g1 * y.reshape(B, T, D)

        h = (rms_norm(xs, norm_g[l, 1]) * (1.0 + sc2) + sh2).reshape(B * T, D)
        out = moe_ffn(h, moe_wr, moe_br, moe_w1, moe_b1, moe_w2, moe_b2, l)
        xs = xs + g2 * out.reshape(B, T, D)

    return rms_norm(xs[:, n_ctx:], final_g)
```

```python
import functools
import math

import jax
import jax.numpy as jnp
from jax import lax
from jax.experimental import pallas as pl
from jax.experimental.pallas import tpu as pltpu

F32 = jnp.float32
BF16 = jnp.bfloat16

EPS = 1e-6
GRID_W = 64

HY_ORDER = 2
HY_EMB = 33
HY_BANDS = (HY_EMB - 1) // 2
HY_DECAY_TARGET = 1e-2
HY_FAST_DECAY = 0.3
HY_SLOW_DECAY = 1.5

GM_GROUPS = 8
GM_CHUNK = 128

DA_HEADS = 8
DA_HEAD_DIM = 128
ROPE_AXIS_DIM = DA_HEAD_DIM // 2
ROPE_THETA = 10000.0

TOP_K = 4
SWIGLU_ALPHA = 1.702
SWIGLU_LIMIT = 7.0

LANES = 128
MIB = 1024 * 1024
VMEM_LIMIT_BYTES = 58 * MIB
ADA_ROWS = 8


def _params(n_grid_dims, vmem=VMEM_LIMIT_BYTES):
    return pltpu.CompilerParams(dimension_semantics=("arbitrary",) * n_grid_dims, vmem_limit_bytes=vmem)


def _tile(n, want, align=LANES):
    if n <= want:
        return n
    return next(t for t in range(want - want % align, 0, -align) if n % t == 0)


class Stream:
    def __init__(self, B, L, n_ctx):
        assert B + 1 <= ADA_ROWS
        self.B, self.L, self.n_ctx = B, L, n_ctx
        self.n_lat = B * L
        self.rows = B * L + B * n_ctx

    def row_tile(self, want):
        return _tile(math.gcd(self.L, self.B * self.n_ctx), want)

    def mod_row(self, row0):
        return jnp.where(row0 < self.n_lat, row0 // self.L, self.B)


def _mm_kernel(a_ref, w_ref, o_ref):
    o_ref[...] = jnp.dot(a_ref[...].astype(BF16), w_ref[...].astype(BF16),
                         preferred_element_type=F32).astype(o_ref.dtype)


def matmul(a, w, layer, *, tm=512, tn=512):
    M, K = a.shape
    N = w.shape[2]
    tm, tn = _tile(M, tm, 8), _tile(N, tn)
    return pl.pallas_call(
        _mm_kernel,
        grid=(N // tn, M // tm),
        in_specs=[pl.BlockSpec((tm, K), lambda j, i: (i, 0)),
                  pl.BlockSpec((None, K, tn), lambda j, i: (layer, 0, j))],
        out_specs=pl.BlockSpec((tm, tn), lambda j, i: (i, j)),
        out_shape=jax.ShapeDtypeStruct((M, N), F32),
        compiler_params=_params(2),
    )(a, w)


def _modulated_norm(x, g, ada_ref, row, shift_col, scale_col):
    D = x.shape[1]
    y = x * lax.rsqrt(jnp.mean(x * x, axis=-1, keepdims=True) + EPS) * g
    scale = ada_ref[pl.ds(row, 1), scale_col * D:(scale_col + 1) * D]
    shift = ada_ref[pl.ds(row, 1), shift_col * D:(shift_col + 1) * D]
    return y * (1.0 + scale) + shift


def _norm_mm_kernel(*refs, stream, shift_col, scale_col, rope):
    if rope is None:
        x_ref, ada_ref, g_ref, w_ref, o_ref, h_scr = refs
    else:
        x_ref, ada_ref, g_ref, w_ref, cos_ref, sin_ref, o_ref, h_scr = refs
    i, j = pl.program_id(0), pl.program_id(1)
    tm, tn = o_ref.shape

    @pl.when(j == 0)
    def _():
        h = _modulated_norm(x_ref[...], g_ref[...], ada_ref, stream.mod_row(i * tm), shift_col, scale_col)
        h_scr[...] = h.astype(BF16)

    y = jnp.dot(h_scr[...], w_ref[...], preferred_element_type=F32)
    if rope is None:
        o_ref[...] = y.astype(o_ref.dtype)
        return

    n_rope_tiles, n_q_tiles, q_scale = rope

    @pl.when(j < n_rope_tiles)
    def _():
        scale = jnp.where(j < n_q_tiles, q_scale, 1.0)
        lane = lax.broadcasted_iota(jnp.int32, (tm, DA_HEAD_DIM), 1)
        first_half = (lane % ROPE_AXIS_DIM) < ROPE_AXIS_DIM // 2
        for hh in range(tn // DA_HEAD_DIM):
            cols = slice(hh * DA_HEAD_DIM, (hh + 1) * DA_HEAD_DIM)
            seg = y[:, cols]
            partner = jnp.where(first_half,
                                pltpu.roll(seg, DA_HEAD_DIM - ROPE_AXIS_DIM // 2, 1),
                                pltpu.roll(seg, ROPE_AXIS_DIM // 2, 1))
            o_ref[:, cols] = ((seg * cos_ref[...] + partner * sin_ref[...]) * scale).astype(o_ref.dtype)

    @pl.when(j >= n_rope_tiles)
    def _():
        o_ref[...] = y.astype(o_ref.dtype)


def rope_tables(L, tile_rows):
    pos = jnp.arange(L)
    inv = ROPE_THETA ** (-jnp.arange(0, ROPE_AXIS_DIM, 2, dtype=F32) / ROPE_AXIS_DIM)
    ar = (pos // GRID_W).astype(F32)[:, None] * inv
    ac = (pos % GRID_W).astype(F32)[:, None] * inv
    cos = jnp.concatenate([jnp.cos(ar), jnp.cos(ar), jnp.cos(ac), jnp.cos(ac)], axis=-1)
    sin = jnp.concatenate([-jnp.sin(ar), jnp.sin(ar), -jnp.sin(ac), jnp.sin(ac)], axis=-1)
    cos = jnp.concatenate([cos, jnp.ones((tile_rows, DA_HEAD_DIM), F32)], axis=0)
    sin = jnp.concatenate([sin, jnp.zeros((tile_rows, DA_HEAD_DIM), F32)], axis=0)
    return cos, sin


def norm_matmul(x, ada, g, w, layer, stream, *, shift_col, scale_col, out_dtype, rope_cols=None, tm=1024, tn=512):
    M, D = x.shape
    N = w.shape[2]
    tm, tn = stream.row_tile(tm), _tile(N, tn)
    in_specs = [pl.BlockSpec((tm, D), lambda i, j: (i, 0)),
                pl.BlockSpec(ada.shape, lambda i, j: (0, 0)),
                pl.BlockSpec((1, D), lambda i, j: (0, 0)),
                pl.BlockSpec((None, D, tn), lambda i, j: (layer, 0, j))]
    args = [x, ada, g.reshape(1, D), w]
    rope = None
    if rope_cols is not None:
        n_rope, n_q, q_scale = rope_cols
        assert n_rope % tn == 0 and n_q % tn == 0 and tn % DA_HEAD_DIM == 0
        rope = (n_rope // tn, n_q // tn, q_scale)
        lat_tiles, tiles_per_sample = stream.n_lat // tm, stream.L // tm
        table_spec = pl.BlockSpec(
            (tm, DA_HEAD_DIM), lambda i, j: (jnp.where(i < lat_tiles, i % tiles_per_sample, tiles_per_sample), 0))
        in_specs += [table_spec, table_spec]
        args += list(rope_tables(stream.L, tm))
    kernel = functools.partial(_norm_mm_kernel, stream=stream, shift_col=shift_col, scale_col=scale_col, rope=rope)
    return pl.pallas_call(
        kernel,
        grid=(M // tm, N // tn),
        in_specs=in_specs,
        out_specs=pl.BlockSpec((tm, tn), lambda i, j: (i, j)),
        out_shape=jax.ShapeDtypeStruct((M, N), out_dtype),
        scratch_shapes=[pltpu.VMEM((tm, D), BF16)],
        compiler_params=_params(2),
    )(*args)


def _mm_res_kernel(*refs, n_a, stream):
    a_refs, w_refs = refs[:n_a], refs[n_a:2 * n_a]
    x_ref, gate_ref, o_ref = refs[2 * n_a:]
    tm = x_ref.shape[0]
    y = jnp.dot(a_refs[0][...], w_refs[0][...], preferred_element_type=F32)
    for a_ref, w_ref in zip(a_refs[1:], w_refs[1:]):
        y = y + jnp.dot(a_ref[...], w_ref[...], preferred_element_type=F32)
    gate = gate_ref[pl.ds(stream.mod_row(pl.program_id(0) * tm), 1), :]
    o_ref[...] = x_ref[...] + gate * y


def matmul_residual(a_list, w, layer, x, ada, gate_col, stream, *, tm=1024, tn=512):
    M, D = x.shape
    n_a = len(a_list)
    ka = a_list[0].shape[1]
    assert all(a.shape == (M, ka) for a in a_list) and w.shape[1:] == (n_a * ka, D)
    tm, tn = stream.row_tile(tm), _tile(D, tn)
    gate_block0 = gate_col * (D // tn)
    kernel = functools.partial(_mm_res_kernel, n_a=n_a, stream=stream)
    a_specs = [pl.BlockSpec((tm, ka), lambda i, j: (i, 0)) for _ in a_list]
    w_specs = [pl.BlockSpec((None, ka, tn), functools.partial(lambda i, j, r: (layer, r, j), r=r)) for r in range(n_a)]
    return pl.pallas_call(
        kernel,
        grid=(M // tm, D // tn),
        in_specs=a_specs + w_specs + [pl.BlockSpec((tm, tn), lambda i, j: (i, j)),
                                      pl.BlockSpec((ADA_ROWS, tn), lambda i, j: (0, gate_block0 + j))],
        out_specs=pl.BlockSpec((tm, tn), lambda i, j: (i, j)),
        out_shape=jax.ShapeDtypeStruct((M, D), F32),
        compiler_params=_params(2),
    )(*a_list, *([w] * n_a), x, ada)


def _lmm_kernel(m_ref, u_ref, o_ref):
    o_ref[...] = jnp.dot(m_ref[...], u_ref[...].astype(BF16), preferred_element_type=F32)


def left_matmul(mat, u, *, tm=512, tn=512):
    R, K = mat.shape
    B, _, C = u.shape
    tm, tn = _tile(R, tm), _tile(C, tn)
    return pl.pallas_call(
        _lmm_kernel,
        grid=(B, C // tn, R // tm),
        in_specs=[pl.BlockSpec((tm, K), lambda b, j, i: (i, 0)),
                  pl.BlockSpec((None, K, tn), lambda b, j, i: (b, 0, j))],
        out_specs=pl.BlockSpec((None, tm, tn), lambda b, j, i: (b, i, j)),
        out_shape=jax.ShapeDtypeStruct((B, R, C), F32),
        compiler_params=_params(3),
    )(mat, u)


def dft_matrices(L):
    n = 2 * L
    r = lax.broadcasted_iota(jnp.int32, (n, L), 0)
    t = lax.broadcasted_iota(jnp.int32, (n, L), 1)
    is_cos = r <= L
    f = jnp.where(is_cos, r, r - L)
    ang = ((f * t) % n).astype(F32) * (2.0 * math.pi / n)
    fwd = jnp.where(is_cos, jnp.cos(ang), -jnp.sin(ang))
    weight = jnp.where((r == 0) | (r == L), 1.0 / n, 2.0 / n)
    return fwd.astype(BF16), (fwd * weight).T.astype(BF16)


def hyena_filters(L, w1, b1, w2, b2, w3, b3, w4, width):
    hp = lax.Precision.HIGHEST
    t = jnp.linspace(0.0, 1.0, L, dtype=F32)[:, None]
    w = 2.0 * math.pi * jnp.arange(L, dtype=F32)[:, None] / L
    f = jnp.linspace(1e-4, HY_BANDS - 1, HY_BANDS, dtype=F32)[None]
    z = jnp.concatenate([t, jnp.cos(f * w), -jnp.sin(f * w)], axis=-1)
    h = jnp.sin(jnp.dot(z, w1, precision=hp) + b1)
    h = jnp.sin(jnp.dot(h, w2, precision=hp) + b2)
    h = jnp.sin(jnp.dot(h, w3, precision=hp) + b3)
    h = jnp.dot(h, w4, precision=hp).reshape(L, HY_ORDER, 2, width)
    max_decay = math.log(HY_DECAY_TARGET) / HY_FAST_DECAY
    min_decay = math.log(HY_DECAY_TARGET) / HY_SLOW_DECAY
    deltas = jnp.abs(jnp.linspace(min_decay, max_decay, HY_ORDER * width, dtype=F32)).reshape(HY_ORDER, 1, width)
    h = h * jnp.exp(-t[:, :, None, None] * deltas)
    fwd, bwd = h[:, :, 0], h[:, :, 1]
    bwd = bwd.at[0].set(0.0)
    scale = lax.rsqrt(jnp.sum(fwd * fwd, axis=0) + jnp.sum(bwd * bwd, axis=0) + EPS)
    return (fwd * scale).reshape(L, -1), (bwd * scale).reshape(L, -1)


def filter_spectrum(L, filt, width, dft_fwd):
    fwd, bwd = hyena_filters(L, *filt, width)
    spec = left_matmul(dft_fwd, jnp.stack([fwd, bwd]))
    r = lax.broadcasted_iota(jnp.int32, (2 * L, 1), 0)
    return spec[0] + jnp.where(r <= L, 1.0, -1.0) * spec[1]


def long_conv(u, k_spec, skip, dft_fwd, dft_inv):
    L = u.shape[1]
    z = left_matmul(dft_fwd, u)
    za, zb = z[:, :L], z[:, L:]
    ka, kb = k_spec[None, :L], k_spec[None, L:]
    first = (lax.broadcasted_iota(jnp.int32, (1, L, 1), 1) == 0)
    bb = zb * kb
    ya = za * ka - jnp.where(first, 0.0, bb)
    yb = jnp.where(first, bb, za * kb + zb * ka)
    y = left_matmul(dft_inv, jnp.concatenate([ya, yb], axis=1))
    return y + u * skip


def short_conv3(u, w, b):
    up = jnp.pad(u, ((0, 0), (1, 1), (0, 0)))
    return up[:, :-2] * w[0] + up[:, 1:-1] * w[1] + up[:, 2:] * w[2] + b


def hyena_mixer(p, conv_w, conv_b, filt, skip):
    L = p.shape[1]
    width = p.shape[2] // (HY_ORDER + 1)
    v, x1, x2 = jnp.split(short_conv3(p, conv_w, conv_b), HY_ORDER + 1, axis=-1)
    dft_fwd, dft_inv = dft_matrices(L)
    k_spec = filter_spectrum(L, filt, width, dft_fwd)
    z = x1 * long_conv(v, k_spec[:, :width], skip[0], dft_fwd, dft_inv)
    return x2 * long_conv(z, k_spec[:, width:], skip[1], dft_fwd, dft_inv)


def _gmlp_kernel(u_ref, v_ref, g_ref, ws_ref, bs_ref, o_ref):
    v = v_ref[...]
    vc = v - jnp.mean(v, axis=-1, keepdims=True)
    vn = vc * lax.rsqrt(jnp.mean(vc * vc, axis=-1, keepdims=True) + EPS) * g_ref[...]
    vn = vn.astype(BF16)
    gd = v.shape[1] // GM_GROUPS
    for g in range(GM_GROUPS):
        cols = slice(g * gd, (g + 1) * gd)
        s = jnp.dot(ws_ref[g].astype(BF16), vn[:, cols], preferred_element_type=F32)
        o_ref[:, cols] = (u_ref[:, cols] * (s + bs_ref[:, cols])).astype(o_ref.dtype)


def chunk_gmlp(p, col0, width, ln_g, ws, bs, layer):
    M = p.shape[0]
    assert col0 % width == 0 and M % GM_CHUNK == 0
    cb = col0 // width
    bs_full = jnp.repeat(bs[layer].T, width // GM_GROUPS, axis=1)
    return pl.pallas_call(
        _gmlp_kernel,
        grid=(M // GM_CHUNK,),
        in_specs=[pl.BlockSpec((GM_CHUNK, width), lambda n: (n, cb)),
                  pl.BlockSpec((GM_CHUNK, width), lambda n: (n, cb + 1)),
                  pl.BlockSpec((None, 1, width), lambda n: (layer, 0, 0)),
                  pl.BlockSpec((None, GM_GROUPS, GM_CHUNK, GM_CHUNK), lambda n: (layer, 0, 0, 0)),
                  pl.BlockSpec((GM_CHUNK, width), lambda n: (0, 0))],
        out_specs=pl.BlockSpec((GM_CHUNK, width), lambda n: (n, 0)),
        out_shape=jax.ShapeDtypeStruct((M, width), BF16),
        compiler_params=_params(1),
    )(p, p, ln_g.reshape(ln_g.shape[0], 1, width), ws, bs_full)


def _attend(lam, q, kv_refs, g_ref, o_ref, out_scale):
    d = q.shape[1] // 2
    maps = []
    for m in range(2):
        cols = slice(m * d, (m + 1) * d)
        ss = [lax.dot_general(q[:, cols], k_ref[:, cols], (((1,), (1,)), ((), ())), preferred_element_type=F32)
              for k_ref, _ in kv_refs]
        mx = functools.reduce(jnp.maximum, [jnp.max(s, axis=-1, keepdims=True) for s in ss])
        es = [jnp.exp(s - mx) for s in ss]
        total = functools.reduce(jnp.add, [jnp.sum(e, axis=-1, keepdims=True) for e in es])
        maps.append((es, total))
    (e0, l0), (e1, l1) = maps
    r0, r1 = 1.0 / l0, lam / l1
    o = None
    for piece, (_, v_ref) in enumerate(kv_refs):
        a = (e0[piece] * r0 - e1[piece] * r1).astype(BF16)
        pv = jnp.dot(a, v_ref[...], preferred_element_type=F32)
        o = pv if o is None else o + pv
    o = o * lax.rsqrt(jnp.mean(o * o, axis=-1, keepdims=True) + EPS)
    o_ref[...] = (o * (g_ref[...] * out_scale)).astype(o_ref.dtype)


def _attn_kernel(lam_ref, q_ref, kl_ref, vl_ref, kc_ref, vc_ref, g_ref, o_ref, *, n_lat_q, out_scale):
    i = pl.program_id(2)
    lam = lam_ref[0]

    @pl.when(i < n_lat_q)
    def _():
        _attend(lam, q_ref[...], [(kl_ref, vl_ref), (kc_ref, vc_ref)], g_ref, o_ref, out_scale)

    @pl.when(i >= n_lat_q)
    def _():
        _attend(lam, q_ref[...], [(kc_ref, vc_ref)], g_ref, o_ref, out_scale)


def diff_attention(qkv, lam, subln_g, layer, out_scale, stream, *, tq=256):
    M = qkv.shape[0]
    B, L, n_ctx = stream.B, stream.L, stream.n_ctx
    hw = 2 * DA_HEAD_DIM
    W = DA_HEADS * hw
    assert qkv.shape[1] == 3 * W
    tq = _tile(math.gcd(L, n_ctx), tq, 8)
    n_lat_q, n_ctx_q = L // tq, n_ctx // tq
    assert (B * L) % n_ctx == 0
    ctx_block0 = (B * L) // n_ctx

    def q_index(b, h, i):
        return (jnp.where(i < n_lat_q, b * n_lat_q + i, B * n_lat_q + b * n_ctx_q + (i - n_lat_q)), h)

    kernel = functools.partial(_attn_kernel, n_lat_q=n_lat_q, out_scale=out_scale)
    return pl.pallas_call(
        kernel,
        grid=(B, DA_HEADS, n_lat_q + n_ctx_q),
        in_specs=[pl.BlockSpec(memory_space=pltpu.SMEM),
                  pl.BlockSpec((tq, hw), q_index),
                  pl.BlockSpec((L, hw), lambda b, h, i: (b, DA_HEADS + h)),
                  pl.BlockSpec((L, hw), lambda b, h, i: (b, 2 * DA_HEADS + h)),
                  pl.BlockSpec((n_ctx, hw), lambda b, h, i: (ctx_block0 + b, DA_HEADS + h)),
                  pl.BlockSpec((n_ctx, hw), lambda b, h, i: (ctx_block0 + b, 2 * DA_HEADS + h)),
                  pl.BlockSpec((None, 1, hw), lambda b, h, i: (layer, 0, 0))],
        out_specs=pl.BlockSpec((tq, hw), q_index),
        out_shape=jax.ShapeDtypeStruct((M, W), BF16),
        compiler_params=_params(3),
    )(lam, qkv, qkv, qkv, qkv, qkv, subln_g.reshape(subln_g.shape[0], 1, hw))


def _router_kernel(x_ref, ada_ref, g_ref, wr_ref, br_ref, h_ref, ti_ref, tg_ref, *, stream, shift_col, scale_col):
    tm = x_ref.shape[0]
    n_exp = wr_ref.shape[1]
    row = stream.mod_row(pl.program_id(0) * tm)
    h = _modulated_norm(x_ref[...], g_ref[...], ada_ref, row, shift_col, scale_col)
    h_ref[...] = h.astype(BF16)
    logits = jnp.dot(h, wr_ref[...], precision=lax.Precision.HIGHEST, preferred_element_type=F32) + br_ref[...]

    lane = lax.broadcasted_iota(jnp.int32, logits.shape, 1)
    out_lane = lax.broadcasted_iota(jnp.int32, ti_ref.shape, 1)
    top_i = jnp.zeros(ti_ref.shape, jnp.int32)
    top_v = []
    vals = logits
    for k in range(TOP_K):
        m = jnp.max(vals, axis=-1, keepdims=True)
        sel = jnp.min(jnp.where(vals == m, lane, n_exp), axis=-1, keepdims=True)
        top_i = jnp.where(out_lane == k, sel, top_i)
        top_v.append(m)
        vals = jnp.where(lane == sel, -jnp.inf, vals)
    ti_ref[...] = top_i

    es = [jnp.exp(v - top_v[0]) for v in top_v]
    total = functools.reduce(jnp.add, es)
    gates = jnp.zeros(tg_ref.shape, F32)
    for k in range(TOP_K):
        gates = jnp.where(out_lane == k, es[k] / total, gates)
    tg_ref[...] = gates


def route(x, n_rows, ada, g, wr, br, layer, stream, *, shift_col, scale_col, tm=256):
    D = x.shape[1]
    E = wr.shape[2]
    tm = stream.row_tile(tm)
    assert n_rows % tm == 0
    kernel = functools.partial(_router_kernel, stream=stream, shift_col=shift_col, scale_col=scale_col)
    h, top_i, gates = pl.pallas_call(
        kernel,
        grid=(n_rows // tm,),
        in_specs=[pl.BlockSpec((tm, D), lambda i: (i, 0)),
                  pl.BlockSpec(ada.shape, lambda i: (0, 0)),
                  pl.BlockSpec((1, D), lambda i: (0, 0)),
                  pl.BlockSpec((None, D, E), lambda i: (layer, 0, 0)),
                  pl.BlockSpec((None, 1, E), lambda i: (layer, 0, 0))],
        out_specs=[pl.BlockSpec((tm, D), lambda i: (i, 0)),
                   pl.BlockSpec((tm, LANES), lambda i: (i, 0)),
                   pl.BlockSpec((tm, LANES), lambda i: (i, 0))],
        out_shape=[jax.ShapeDtypeStruct((n_rows, D), BF16),
                   jax.ShapeDtypeStruct((n_rows, LANES), jnp.int32),
                   jax.ShapeDtypeStruct((n_rows, LANES), F32)],
        compiler_params=_params(1),
    )(x, ada, g.reshape(1, D), wr, br.reshape(br.shape[0], 1, E))
    return h, top_i[:, :TOP_K], gates[:, :TOP_K]


def _moe_kernel(te_ref, nt_ref, x_ref, g_ref, w1_ref, b1_ref, w2_ref, b2_ref, o_ref, w1_scr, w2_scr, *, n_chunks):
    t = pl.program_id(0)
    ff = w2_ref.shape[0]
    fc = ff // n_chunks
    used = t < nt_ref[0]
    new_expert = jnp.logical_or(t == 0, te_ref[t] != te_ref[jnp.maximum(t - 1, 0)])

    @pl.when(jnp.logical_and(used, new_expert))
    def _():
        for c in range(2 * n_chunks):
            cols = slice(c * fc, (c + 1) * fc)
            w1_scr[:, cols] = w1_ref[:, cols].astype(BF16)
        for c in range(n_chunks):
            rows = slice(c * fc, (c + 1) * fc)
            w2_scr[rows, :] = w2_ref[rows, :].astype(BF16)

    @pl.when(used)
    def _():
        x = x_ref[...]
        y = jnp.zeros(o_ref.shape, F32)
        for c in range(n_chunks):
            glu_cols = slice(c * fc, (c + 1) * fc)
            lin_cols = slice(ff + c * fc, ff + (c + 1) * fc)
            glu = jnp.dot(x, w1_scr[:, glu_cols], preferred_element_type=F32) + b1_ref[:, glu_cols]
            lin = jnp.dot(x, w1_scr[:, lin_cols], preferred_element_type=F32) + b1_ref[:, lin_cols]
            glu = jnp.minimum(glu, SWIGLU_LIMIT)
            lin = jnp.clip(lin, -SWIGLU_LIMIT, SWIGLU_LIMIT)
            act = glu * jax.nn.sigmoid(SWIGLU_ALPHA * glu) * (lin + 1.0)
            y = y + jnp.dot(act.astype(BF16), w2_scr[glu_cols, :], preferred_element_type=F32)
        o_ref[...] = ((y + b2_ref[...]) * g_ref[...]).astype(o_ref.dtype)

    @pl.when(jnp.logical_not(used))
    def _():
        o_ref[...] = jnp.zeros(o_ref.shape, o_ref.dtype)


def moe_experts(xs, row_gate, tile_expert, n_tiles_used, w1, b1, w2, b2, layer, *, tm):
    P, D = xs.shape
    E, _, F2 = w1.shape[1:]
    ff = F2 // 2
    n_tiles = P // tm
    kernel = functools.partial(_moe_kernel, n_chunks=3)
    grid_spec = pltpu.PrefetchScalarGridSpec(
        num_scalar_prefetch=2,
        grid=(n_tiles,),
        in_specs=[pl.BlockSpec((tm, D), lambda t, te, nt: (t, 0)),
                  pl.BlockSpec((tm, 1), lambda t, te, nt: (t, 0)),
                  pl.BlockSpec((None, None, D, F2), lambda t, te, nt: (layer, te[t], 0, 0)),
                  pl.BlockSpec((None, None, 1, F2), lambda t, te, nt: (layer, te[t], 0, 0)),
                  pl.BlockSpec((None, None, ff, D), lambda t, te, nt: (layer, te[t], 0, 0)),
                  pl.BlockSpec((None, None, 1, D), lambda t, te, nt: (layer, te[t], 0, 0))],
        out_specs=pl.BlockSpec((tm, D), lambda t, te, nt: (t, 0)),
        scratch_shapes=[pltpu.VMEM((D, F2), BF16), pltpu.VMEM((ff, D), BF16)],
    )
    return pl.pallas_call(
        kernel,
        grid_spec=grid_spec,
        out_shape=jax.ShapeDtypeStruct((P, D), BF16),
        compiler_params=_params(1),
    )(tile_expert, n_tiles_used, xs, row_gate, w1, b1.reshape(b1.shape[0], E, 1, F2), w2,
      b2.reshape(b2.shape[0], E, 1, D))


def moe_ffn(h, top_i, gates, w1, b1, w2, b2, layer, *, tm=256):
    N, D = h.shape
    E = w1.shape[1]
    P = N * TOP_K
    n_tiles = P // tm + E
    i32 = jnp.int32

    pair_e = top_i.reshape(P)
    pair_ids = jnp.arange(P, dtype=i32)
    _, order = lax.sort((pair_e, pair_ids), num_keys=1, is_stable=True)
    _, rank = lax.sort((order, pair_ids), num_keys=1)
    counts = jnp.sum((pair_e[:, None] == jnp.arange(E, dtype=i32)[None, :]).astype(i32), axis=0)
    tiles_per = (counts + tm - 1) // tm
    tile_end = jnp.cumsum(tiles_per)
    tile_start = tile_end - tiles_per
    start_sorted = jnp.cumsum(counts) - counts
    n_used = tile_end[-1:]
    tile_ids = jnp.minimum(jnp.arange(n_tiles, dtype=i32), n_used - 1)
    tile_expert = jnp.sum((tile_ids[:, None] >= tile_end[None, :]).astype(i32), axis=1)

    row_e = jnp.repeat(tile_expert, tm)
    row = jnp.arange(n_tiles * tm, dtype=i32)
    off = row - tile_start[row_e] * tm
    valid = (off < counts[row_e]) & (row < n_used * tm)
    row_pair = order[jnp.clip(start_sorted[row_e] + off, 0, P - 1)]
    row_token = jnp.where(valid, row_pair // TOP_K, 0)
    row_gate = jnp.where(valid, gates.reshape(P)[row_pair], 0.0)
    pair_row = (tile_start[pair_e] * tm + rank - start_sorted[pair_e]).reshape(N, TOP_K)

    xs = h[row_token]
    ys = moe_experts(xs, row_gate[:, None], tile_expert, n_used, w1, b1, w2, b2, layer, tm=tm)
    return [ys[pair_row[:, k]] for k in range(TOP_K)]


def _combine_kernel(x_ref, gate_ref, *refs, stream):
    y_refs, o_ref = refs[:-1], refs[-1]
    tm = x_ref.shape[0]
    gate = gate_ref[pl.ds(stream.mod_row(pl.program_id(0) * tm), 1), :]
    y = functools.reduce(jnp.add, [y_ref[...].astype(F32) for y_ref in y_refs])
    o_ref[...] = x_ref[...] + gate * y


def combine_residual(x, ys, ada, gate_col, stream, *, tm=512):
    n, D = ys[0].shape
    tm = stream.row_tile(tm)
    assert n % tm == 0
    row_spec = pl.BlockSpec((tm, D), lambda i: (i, 0))
    return pl.pallas_call(
        functools.partial(_combine_kernel, stream=stream),
        grid=(n // tm,),
        in_specs=[row_spec, pl.BlockSpec((ADA_ROWS, D), lambda i: (0, gate_col))] + [row_spec] * len(ys),
        out_specs=row_spec,
        out_shape=jax.ShapeDtypeStruct((n, D), F32),
        compiler_params=_params(1),
    )(x, ada, *ys)


def rms_norm(x, g):
    return x * lax.rsqrt(jnp.mean(x * x, axis=-1, keepdims=True) + EPS) * g


def kernel(x, c, ctx, c_ctx, ada_w, ada_b, norm_g, final_g, ev_w_in, ev_w_out, hy_conv_w, hy_conv_b,
           hy_f_w1, hy_f_b1, hy_f_w2, hy_f_b2, hy_f_w3, hy_f_b3, hy_f_w4, hy_skip, gm_ln_g, gm_ws, gm_bs,
           od_w_qkv, od_w_out, od_lambda, od_subln_g, moe_wr, moe_br, moe_w1, moe_b1, moe_w2, moe_b2):
    B, L, D = x.shape
    n_ctx = ctx.shape[1]
    depth = ada_w.shape[0]
    hy_width = hy_skip.shape[2]
    hy_split = (HY_ORDER + 1) * hy_width
    gm_width = gm_ln_g.shape[1]
    stream = Stream(B, L, n_ctx)
    n_lat, M = stream.n_lat, stream.rows

    xs = jnp.concatenate([x.reshape(n_lat, D), ctx.reshape(B * n_ctx, D)], axis=0)
    cond = jnp.concatenate([jax.nn.silu(c), jax.nn.silu(c_ctx)[None], jnp.zeros((ADA_ROWS - B - 1, D), F32)], axis=0)
    ev_w_in, ev_w_out, od_w_qkv, od_w_out = (w.astype(BF16) for w in (ev_w_in, ev_w_out, od_w_qkv, od_w_out))

    for l in range(depth):
        i = l // 2
        last = l == depth - 1
        ada = matmul(cond.astype(BF16), ada_w, l, tm=ADA_ROWS, tn=1024) + ada_b[l]

        if l % 2 == 0:
            p = norm_matmul(xs, ada, norm_g[l, 0], ev_w_in, i, stream, shift_col=0, scale_col=1, out_dtype=F32)
            filt = (hy_f_w1[i], hy_f_b1[i], hy_f_w2[i], hy_f_b2[i], hy_f_w3[i], hy_f_b3[i], hy_f_w4[i])
            p_lat = p[:n_lat, :hy_split].reshape(B, L, hy_split)
            p_ctx = p[n_lat:, :hy_split].reshape(B, n_ctx, hy_split)
            y_a = jnp.concatenate(
                [hyena_mixer(p_lat, hy_conv_w[i], hy_conv_b[i], filt, hy_skip[i]).astype(BF16).reshape(n_lat, -1),
                 hyena_mixer(p_ctx, hy_conv_w[i], hy_conv_b[i], filt, hy_skip[i]).astype(BF16).reshape(M - n_lat, -1)],
                axis=0)
            y_b = chunk_gmlp(p, hy_split, gm_width, gm_ln_g, gm_ws, gm_bs, i)
            xs = matmul_residual([y_a, y_b], ev_w_out, i, xs, ada, 2, stream)
        else:
            lam_init = 0.8 - 0.6 * math.exp(-0.3 * l)
            W = od_w_qkv.shape[2] // 3
            qkv = norm_matmul(xs, ada, norm_g[l, 0], od_w_qkv, i, stream, shift_col=0, scale_col=1, out_dtype=BF16,
                              rope_cols=(2 * W, W, 1.0 / math.sqrt(DA_HEAD_DIM)))
            lp = od_lambda[i]
            lam = (jnp.exp(jnp.sum(lp[0] * lp[1])) - jnp.exp(jnp.sum(lp[2] * lp[3])) + lam_init).reshape(1)
            o = diff_attention(qkv, lam, od_subln_g, i, 1.0 - lam_init, stream)
            xs = matmul_residual([o], od_w_out, i, xs, ada, 2, stream)

        n_tok = n_lat if last else M
        h, top_i, gates = route(xs, n_tok, ada, norm_g[l, 1], moe_wr, moe_br, l, stream, shift_col=3, scale_col=4)
        outs = moe_ffn(h, top_i, gates, moe_w1, moe_b1, moe_w2, moe_b2, l)
        xs = combine_residual(xs, outs, ada, 5, stream)

    return rms_norm(xs[:n_lat].reshape(B, L, D), final_g)
```

```python
import functools
import math

import jax
import jax.numpy as jnp
from jax import lax
from jax.experimental import pallas as pl
from jax.experimental.pallas import tpu as pltpu

F32 = jnp.float32
BF16 = jnp.bfloat16

EPS = 1e-6
GRID_W = 64

HY_ORDER = 2
HY_EMB = 33
HY_BANDS = (HY_EMB - 1) // 2
HY_DECAY_TARGET = 1e-2
HY_FAST_DECAY = 0.3
HY_SLOW_DECAY = 1.5

GM_GROUPS = 8
GM_CHUNK = 128

DA_HEADS = 8
DA_HEAD_DIM = 128
ROPE_AXIS_DIM = DA_HEAD_DIM // 2
ROPE_THETA = 10000.0

TOP_K = 4
SWIGLU_ALPHA = 1.702
SWIGLU_LIMIT = 7.0

LANES = 128
MIB = 1024 * 1024
VMEM_LIMIT_BYTES = 58 * MIB
ADA_ROWS = 8


def _params(n_grid_dims, vmem=VMEM_LIMIT_BYTES):
    return pltpu.CompilerParams(dimension_semantics=("arbitrary",) * n_grid_dims, vmem_limit_bytes=vmem)


def _tile(n, want, align=LANES):
    if n <= want:
        return n
    return next(t for t in range(want - want % align, 0, -align) if n % t == 0)


class Stream:
    def __init__(self, B, L, n_ctx):
        assert B + 1 <= ADA_ROWS
        self.B, self.L, self.n_ctx = B, L, n_ctx
        self.n_lat = B * L
        self.rows = B * L + B * n_ctx

    def row_tile(self, want):
        return _tile(math.gcd(self.L, self.B * self.n_ctx), want)

    def mod_row(self, row0):
        return jnp.where(row0 < self.n_lat, row0 // self.L, self.B)


def _mm_kernel(a_ref, w_ref, o_ref):
    o_ref[...] = jnp.dot(a_ref[...].astype(BF16), w_ref[...].astype(BF16),
                         preferred_element_type=F32).astype(o_ref.dtype)


def matmul(a, w, layer, *, tm=512, tn=512):
    M, K = a.shape
    N = w.shape[2]
    tm, tn = _tile(M, tm, 8), _tile(N, tn)
    return pl.pallas_call(
        _mm_kernel,
        grid=(N // tn, M // tm),
        in_specs=[pl.BlockSpec((tm, K), lambda j, i: (i, 0)),
                  pl.BlockSpec((None, K, tn), lambda j, i: (layer, 0, j))],
        out_specs=pl.BlockSpec((tm, tn), lambda j, i: (i, j)),
        out_shape=jax.ShapeDtypeStruct((M, N), F32),
        compiler_params=_params(2),
    )(a, w)


def _modulated_norm(x, g, ada_ref, row, shift_col, scale_col):
    D = x.shape[1]
    y = x * lax.rsqrt(jnp.mean(x * x, axis=-1, keepdims=True) + EPS) * g
    scale = ada_ref[pl.ds(row, 1), scale_col * D:(scale_col + 1) * D]
    shift = ada_ref[pl.ds(row, 1), shift_col * D:(shift_col + 1) * D]
    return y * (1.0 + scale) + shift


def _norm_mm_kernel(*refs, stream, shift_col, scale_col, rope):
    if rope is None:
        x_ref, ada_ref, g_ref, w_ref, o_ref, h_scr = refs
    else:
        x_ref, ada_ref, g_ref, w_ref, cos_ref, sin_ref, o_ref, h_scr = refs
    i, j = pl.program_id(0), pl.program_id(1)
    tm, tn = o_ref.shape

    @pl.when(j == 0)
    def _():
        h = _modulated_norm(x_ref[...], g_ref[...], ada_ref, stream.mod_row(i * tm), shift_col, scale_col)
        h_scr[...] = h.astype(BF16)

    y = jnp.dot(h_scr[...], w_ref[...], preferred_element_type=F32)
    if rope is None:
        o_ref[...] = y.astype(o_ref.dtype)
        return

    n_rope_tiles, n_q_tiles, q_scale = rope

    @pl.when(j < n_rope_tiles)
    def _():
        scale = jnp.where(j < n_q_tiles, q_scale, 1.0)
        lane = lax.broadcasted_iota(jnp.int32, (tm, DA_HEAD_DIM), 1)
        first_half = (lane % ROPE_AXIS_DIM) < ROPE_AXIS_DIM // 2
        for hh in range(tn // DA_HEAD_DIM):
            cols = slice(hh * DA_HEAD_DIM, (hh + 1) * DA_HEAD_DIM)
            seg = y[:, cols]
            partner = jnp.where(first_half,
                                pltpu.roll(seg, DA_HEAD_DIM - ROPE_AXIS_DIM // 2, 1),
                                pltpu.roll(seg, ROPE_AXIS_DIM // 2, 1))
            o_ref[:, cols] = ((seg * cos_ref[...] + partner * sin_ref[...]) * scale).astype(o_ref.dtype)

    @pl.when(j >= n_rope_tiles)
    def _():
        o_ref[...] = y.astype(o_ref.dtype)


def rope_tables(L, tile_rows):
    pos = jnp.arange(L)
    inv = ROPE_THETA ** (-jnp.arange(0, ROPE_AXIS_DIM, 2, dtype=F32) / ROPE_AXIS_DIM)
    ar = (pos // GRID_W).astype(F32)[:, None] * inv
    ac = (pos % GRID_W).astype(F32)[:, None] * inv
    cos = jnp.concatenate([jnp.cos(ar), jnp.cos(ar), jnp.cos(ac), jnp.cos(ac)], axis=-1)
    sin = jnp.concatenate([-jnp.sin(ar), jnp.sin(ar), -jnp.sin(ac), jnp.sin(ac)], axis=-1)
    cos = jnp.concatenate([cos, jnp.ones((tile_rows, DA_HEAD_DIM), F32)], axis=0)
    sin = jnp.concatenate([sin, jnp.zeros((tile_rows, DA_HEAD_DIM), F32)], axis=0)
    return cos, sin


def norm_matmul(x, ada, g, w, layer, stream, *, shift_col, scale_col, out_dtype, rope_cols=None, tm=1024, tn=512):
    M, D = x.shape
    N = w.shape[2]
    tm, tn = stream.row_tile(tm), _tile(N, tn)
    in_specs = [pl.BlockSpec((tm, D), lambda i, j: (i, 0)),
                pl.BlockSpec(ada.shape, lambda i, j: (0, 0)),
                pl.BlockSpec((1, D), lambda i, j: (0, 0)),
                pl.BlockSpec((None, D, tn), lambda i, j: (layer, 0, j))]
    args = [x, ada, g.reshape(1, D), w]
    rope = None
    if rope_cols is not None:
        n_rope, n_q, q_scale = rope_cols
        assert n_rope % tn == 0 and n_q % tn == 0 and tn % DA_HEAD_DIM == 0
        rope = (n_rope // tn, n_q // tn, q_scale)
        lat_tiles, tiles_per_sample = stream.n_lat // tm, stream.L // tm
        table_spec = pl.BlockSpec(
            (tm, DA_HEAD_DIM), lambda i, j: (jnp.where(i < lat_tiles, i % tiles_per_sample, tiles_per_sample), 0))
        in_specs += [table_spec, table_spec]
        args += list(rope_tables(stream.L, tm))
    kernel = functools.partial(_norm_mm_kernel, stream=stream, shift_col=shift_col, scale_col=scale_col, rope=rope)
    return pl.pallas_call(
        kernel,
        grid=(M // tm, N // tn),
        in_specs=in_specs,
        out_specs=pl.BlockSpec((tm, tn), lambda i, j: (i, j)),
        out_shape=jax.ShapeDtypeStruct((M, N), out_dtype),
        scratch_shapes=[pltpu.VMEM((tm, D), BF16)],
        compiler_params=_params(2),
    )(*args)


def _mm_res_kernel(*refs, n_a, stream):
    a_refs, w_refs = refs[:n_a], refs[n_a:2 * n_a]
    x_ref, gate_ref, o_ref = refs[2 * n_a:]
    tm = x_ref.shape[0]
    y = jnp.dot(a_refs[0][...], w_refs[0][...], preferred_element_type=F32)
    for a_ref, w_ref in zip(a_refs[1:], w_refs[1:]):
        y = y + jnp.dot(a_ref[...], w_ref[...], preferred_element_type=F32)
    gate = gate_ref[pl.ds(stream.mod_row(pl.program_id(0) * tm), 1), :]
    o_ref[...] = x_ref[...] + gate * y


def matmul_residual(a_list, w, layer, x, ada, gate_col, stream, *, tm=1024, tn=512):
    M, D = x.shape
    n_a = len(a_list)
    ka = a_list[0].shape[1]
    assert all(a.shape == (M, ka) for a in a_list) and w.shape[1:] == (n_a * ka, D)
    tm, tn = stream.row_tile(tm), _tile(D, tn)
    gate_block0 = gate_col * (D // tn)
    kernel = functools.partial(_mm_res_kernel, n_a=n_a, stream=stream)
    a_specs = [pl.BlockSpec((tm, ka), lambda i, j: (i, 0)) for _ in a_list]
    w_specs = [pl.BlockSpec((None, ka, tn), functools.partial(lambda i, j, r: (layer, r, j), r=r)) for r in range(n_a)]
    return pl.pallas_call(
        kernel,
        grid=(M // tm, D // tn),
        in_specs=a_specs + w_specs + [pl.BlockSpec((tm, tn), lambda i, j: (i, j)),
                                      pl.BlockSpec((ADA_ROWS, tn), lambda i, j: (0, gate_block0 + j))],
        out_specs=pl.BlockSpec((tm, tn), lambda i, j: (i, j)),
        out_shape=jax.ShapeDtypeStruct((M, D), F32),
        compiler_params=_params(2),
    )(*a_list, *([w] * n_a), x, ada)


def _lmm_kernel(m_ref, u_ref, o_ref):
    o_ref[...] = jnp.dot(m_ref[...], u_ref[...].astype(BF16), preferred_element_type=F32)


def left_matmul(mat, u, *, tm=512, tn=512):
    R, K = mat.shape
    B, _, C = u.shape
    tm, tn = _tile(R, tm), _tile(C, tn)
    return pl.pallas_call(
        _lmm_kernel,
        grid=(B, C // tn, R // tm),
        in_specs=[pl.BlockSpec((tm, K), lambda b, j, i: (i, 0)),
                  pl.BlockSpec((None, K, tn), lambda b, j, i: (b, 0, j))],
        out_specs=pl.BlockSpec((None, tm, tn), lambda b, j, i: (b, i, j)),
        out_shape=jax.ShapeDtypeStruct((B, R, C), F32),
        compiler_params=_params(3),
    )(mat, u)


def dft_matrices(L):
    n = 2 * L
    r = lax.broadcasted_iota(jnp.int32, (n, L), 0)
    t = lax.broadcasted_iota(jnp.int32, (n, L), 1)
    is_cos = r <= L
    f = jnp.where(is_cos, r, r - L)
    ang = ((f * t) % n).astype(F32) * (2.0 * math.pi / n)
    fwd = jnp.where(is_cos, jnp.cos(ang), -jnp.sin(ang))
    weight = jnp.where((r == 0) | (r == L), 1.0 / n, 2.0 / n)
    return fwd.astype(BF16), (fwd * weight).T.astype(BF16)


def hyena_filters(L, w1, b1, w2, b2, w3, b3, w4, width):
    hp = lax.Precision.HIGHEST
    t = jnp.linspace(0.0, 1.0, L, dtype=F32)[:, None]
    w = 2.0 * math.pi * jnp.arange(L, dtype=F32)[:, None] / L
    f = jnp.linspace(1e-4, HY_BANDS - 1, HY_BANDS, dtype=F32)[None]
    z = jnp.concatenate([t, jnp.cos(f * w), -jnp.sin(f * w)], axis=-1)
    h = jnp.sin(jnp.dot(z, w1, precision=hp) + b1)
    h = jnp.sin(jnp.dot(h, w2, precision=hp) + b2)
    h = jnp.sin(jnp.dot(h, w3, precision=hp) + b3)
    h = jnp.dot(h, w4, precision=hp).reshape(L, HY_ORDER, 2, width)
    max_decay = math.log(HY_DECAY_TARGET) / HY_FAST_DECAY
    min_decay = math.log(HY_DECAY_TARGET) / HY_SLOW_DECAY
    deltas = jnp.abs(jnp.linspace(min_decay, max_decay, HY_ORDER * width, dtype=F32)).reshape(HY_ORDER, 1, width)
    h = h * jnp.exp(-t[:, :, None, None] * deltas)
    fwd, bwd = h[:, :, 0], h[:, :, 1]
    bwd = bwd.at[0].set(0.0)
    scale = lax.rsqrt(jnp.sum(fwd * fwd, axis=0) + jnp.sum(bwd * bwd, axis=0) + EPS)
    return (fwd * scale).reshape(L, -1), (bwd * scale).reshape(L, -1)


def filter_spectrum(L, filt, width, dft_fwd):
    fwd, bwd = hyena_filters(L, *filt, width)
    spec = left_matmul(dft_fwd, jnp.stack([fwd, bwd]))
    r = lax.broadcasted_iota(jnp.int32, (2 * L, 1), 0)
    return spec[0] + jnp.where(r <= L, 1.0, -1.0) * spec[1]


def _dft_mul_kernel(m_ref, u_ref, k_ref, o_ref):
    i = pl.program_id(2)
    tm = o_ref.shape[1]
    u = u_ref[...]
    za = jnp.dot(m_ref[0], u, preferred_element_type=F32)
    zb = jnp.dot(m_ref[1], u, preferred_element_type=F32)
    ka, kb = k_ref[0], k_ref[1]
    first = (i * tm + lax.broadcasted_iota(jnp.int32, za.shape, 0)) == 0
    bb = zb * kb
    o_ref[0] = (za * ka - jnp.where(first, 0.0, bb)).astype(o_ref.dtype)
    o_ref[1] = jnp.where(first, bb, za * kb + zb * ka).astype(o_ref.dtype)


def dft_multiply(dft_fwd, u, u_col0, k_spec, k_col0, n_seq, width, *, tm=512, tn=512):
    L = dft_fwd.shape[1]
    tm, tn = _tile(L, tm), _tile(width, tn)
    assert u_col0 % tn == 0 and k_col0 % tn == 0 and u.shape[0] == n_seq * L
    ucb, kcb = u_col0 // tn, k_col0 // tn
    return pl.pallas_call(
        _dft_mul_kernel,
        grid=(n_seq, width // tn, L // tm),
        in_specs=[pl.BlockSpec((2, tm, L), lambda b, j, i: (0, i, 0)),
                  pl.BlockSpec((L, tn), lambda b, j, i: (b, ucb + j)),
                  pl.BlockSpec((2, tm, tn), lambda b, j, i: (0, i, kcb + j))],
        out_specs=pl.BlockSpec((None, 2, tm, tn), lambda b, j, i: (b, 0, i, j)),
        out_shape=jax.ShapeDtypeStruct((n_seq, 2, L, width), BF16),
        compiler_params=_params(3),
    )(dft_fwd, u, k_spec)


def _idft_gate_kernel(m_ref, y_ref, u_ref, skip_ref, gate_ref, o_ref):
    y = jnp.dot(m_ref[...], y_ref[...], preferred_element_type=F32)
    y = y + u_ref[...].astype(F32) * skip_ref[...]
    o_ref[...] = (gate_ref[...].astype(F32) * y).astype(o_ref.dtype)


def idft_gate(dft_inv, y_spec, u, u_col0, skip, skip_row, gate, gate_col0, *, tm=512, tn=512):
    L = dft_inv.shape[0]
    n_seq, _, width = y_spec.shape
    tm, tn = _tile(L, tm), _tile(width, tn)
    assert u_col0 % tn == 0 and gate_col0 % tn == 0
    ucb, gcb, tiles = u_col0 // tn, gate_col0 // tn, L // tm
    return pl.pallas_call(
        _idft_gate_kernel,
        grid=(n_seq, width // tn, tiles),
        in_specs=[pl.BlockSpec((tm, 2 * L), lambda b, j, i: (i, 0)),
                  pl.BlockSpec((None, 2 * L, tn), lambda b, j, i: (b, 0, j)),
                  pl.BlockSpec((tm, tn), lambda b, j, i: (b * tiles + i, ucb + j)),
                  pl.BlockSpec((None, 1, tn), lambda b, j, i: (skip_row, 0, j)),
                  pl.BlockSpec((tm, tn), lambda b, j, i: (b * tiles + i, gcb + j))],
        out_specs=pl.BlockSpec((tm, tn), lambda b, j, i: (b * tiles + i, j)),
        out_shape=jax.ShapeDtypeStruct((n_seq * L, width), BF16),
        compiler_params=_params(3),
    )(dft_inv, y_spec, u, skip, gate)


def _short_conv_kernel(p_ref, w_ref, b_ref, o_ref):
    x = p_ref[...].astype(F32)
    n = x.shape[0]
    row = lax.broadcasted_iota(jnp.int32, x.shape, 0)
    prev = jnp.where(row == 0, 0.0, pltpu.roll(x, 1, 0))
    nxt = jnp.where(row == n - 1, 0.0, pltpu.roll(x, n - 1, 0))
    o_ref[...] = (prev * w_ref[0:1, :] + x * w_ref[1:2, :] + nxt * w_ref[2:3, :] + b_ref[...]).astype(o_ref.dtype)


def short_conv3(p, row0, n_seq, L, width, w, b, layer, *, tc=256):
    tc = _tile(width, tc)
    assert row0 % L == 0
    rb0 = row0 // L
    return pl.pallas_call(
        _short_conv_kernel,
        grid=(n_seq, width // tc),
        in_specs=[pl.BlockSpec((L, tc), lambda s, j: (rb0 + s, j)),
                  pl.BlockSpec((None, 3, tc), lambda s, j: (layer, 0, j)),
                  pl.BlockSpec((None, 1, tc), lambda s, j: (layer, 0, j))],
        out_specs=pl.BlockSpec((L, tc), lambda s, j: (s, j)),
        out_shape=jax.ShapeDtypeStruct((n_seq * L, width), BF16),
        compiler_params=_params(2),
    )(p, w, b.reshape(b.shape[0], 1, width))


def hyena_mixer(p, row0, n_seq, L, width, conv_w, conv_b, filt, skip, layer):
    vxx = short_conv3(p, row0, n_seq, L, (HY_ORDER + 1) * width, conv_w, conv_b, layer)
    dft_fwd, dft_inv = dft_matrices(L)
    k_spec = filter_spectrum(L, filt, width, dft_fwd).reshape(2, L, HY_ORDER * width)
    dft_fwd = dft_fwd.reshape(2, L, L)
    skip = skip.reshape(-1, 1, width)
    spec = dft_multiply(dft_fwd, vxx, 0, k_spec, 0, n_seq, width).reshape(n_seq, 2 * L, width)
    z = idft_gate(dft_inv, spec, vxx, 0, skip, layer * HY_ORDER, vxx, width)
    spec = dft_multiply(dft_fwd, z, 0, k_spec, width, n_seq, width).reshape(n_seq, 2 * L, width)
    return idft_gate(dft_inv, spec, z, 0, skip, layer * HY_ORDER + 1, vxx, 2 * width)


def _gmlp_kernel(u_ref, v_ref, g_ref, ws_ref, bs_ref, o_ref):
    v = v_ref[...].astype(F32)
    vc = v - jnp.mean(v, axis=-1, keepdims=True)
    vn = vc * lax.rsqrt(jnp.mean(vc * vc, axis=-1, keepdims=True) + EPS) * g_ref[...]
    vn = vn.astype(BF16)
    gd = v.shape[1] // GM_GROUPS
    for g in range(GM_GROUPS):
        cols = slice(g * gd, (g + 1) * gd)
        s = jnp.dot(ws_ref[g].astype(BF16), vn[:, cols], preferred_element_type=F32)
        o_ref[:, cols] = (u_ref[:, cols].astype(F32) * (s + bs_ref[:, cols])).astype(o_ref.dtype)


def chunk_gmlp(p, col0, width, ln_g, ws, bs, layer):
    M = p.shape[0]
    assert col0 % width == 0 and M % GM_CHUNK == 0
    cb = col0 // width
    bs_full = jnp.repeat(bs[layer].T, width // GM_GROUPS, axis=1)
    return pl.pallas_call(
        _gmlp_kernel,
        grid=(M // GM_CHUNK,),
        in_specs=[pl.BlockSpec((GM_CHUNK, width), lambda n: (n, cb)),
                  pl.BlockSpec((GM_CHUNK, width), lambda n: (n, cb + 1)),
                  pl.BlockSpec((None, 1, width), lambda n: (layer, 0, 0)),
                  pl.BlockSpec((None, GM_GROUPS, GM_CHUNK, GM_CHUNK), lambda n: (layer, 0, 0, 0)),
                  pl.BlockSpec((GM_CHUNK, width), lambda n: (0, 0))],
        out_specs=pl.BlockSpec((GM_CHUNK, width), lambda n: (n, 0)),
        out_shape=jax.ShapeDtypeStruct((M, width), BF16),
        compiler_params=_params(1),
    )(p, p, ln_g.reshape(ln_g.shape[0], 1, width), ws, bs_full)


def _attend(lam, q, kv_refs, g_ref, o_ref, out_scale):
    d = q.shape[1] // 2
    maps = []
    for m in range(2):
        cols = slice(m * d, (m + 1) * d)
        ss = [lax.dot_general(q[:, cols], k_ref[:, cols], (((1,), (1,)), ((), ())), preferred_element_type=F32)
              for k_ref, _ in kv_refs]
        mx = functools.reduce(jnp.maximum, [jnp.max(s, axis=-1, keepdims=True) for s in ss])
        es = [jnp.exp2(s - mx) for s in ss]
        total = functools.reduce(jnp.add, [jnp.sum(e, axis=-1, keepdims=True) for e in es])
        maps.append((es, total))
    (e0, l0), (e1, l1) = maps
    r0, r1 = 1.0 / l0, lam / l1
    o = None
    for piece, (_, v_ref) in enumerate(kv_refs):
        a = (e0[piece] * r0 - e1[piece] * r1).astype(BF16)
        pv = jnp.dot(a, v_ref[...], preferred_element_type=F32)
        o = pv if o is None else o + pv
    o = o * lax.rsqrt(jnp.mean(o * o, axis=-1, keepdims=True) + EPS)
    o_ref[...] = (o * (g_ref[...] * out_scale)).astype(o_ref.dtype)


def _attn_kernel(lam_ref, q_ref, kl_ref, vl_ref, kc_ref, vc_ref, g_ref, o_ref, *, n_lat_q, out_scale):
    i = pl.program_id(2)
    lam = lam_ref[0]

    @pl.when(i < n_lat_q)
    def _():
        _attend(lam, q_ref[...], [(kl_ref, vl_ref), (kc_ref, vc_ref)], g_ref, o_ref, out_scale)

    @pl.when(i >= n_lat_q)
    def _():
        _attend(lam, q_ref[...], [(kc_ref, vc_ref)], g_ref, o_ref, out_scale)


def diff_attention(qkv, lam, subln_g, layer, out_scale, stream, *, tq=256):
    M = qkv.shape[0]
    B, L, n_ctx = stream.B, stream.L, stream.n_ctx
    hw = 2 * DA_HEAD_DIM
    W = DA_HEADS * hw
    assert qkv.shape[1] == 3 * W
    tq = _tile(math.gcd(L, n_ctx), tq, 8)
    n_lat_q, n_ctx_q = L // tq, n_ctx // tq
    assert (B * L) % n_ctx == 0
    ctx_block0 = (B * L) // n_ctx

    def q_index(b, h, i):
        return (jnp.where(i < n_lat_q, b * n_lat_q + i, B * n_lat_q + b * n_ctx_q + (i - n_lat_q)), h)

    kernel = functools.partial(_attn_kernel, n_lat_q=n_lat_q, out_scale=out_scale)
    return pl.pallas_call(
        kernel,
        grid=(B, DA_HEADS, n_lat_q + n_ctx_q),
        in_specs=[pl.BlockSpec(memory_space=pltpu.SMEM),
                  pl.BlockSpec((tq, hw), q_index),
                  pl.BlockSpec((L, hw), lambda b, h, i: (b, DA_HEADS + h)),
                  pl.BlockSpec((L, hw), lambda b, h, i: (b, 2 * DA_HEADS + h)),
                  pl.BlockSpec((n_ctx, hw), lambda b, h, i: (ctx_block0 + b, DA_HEADS + h)),
                  pl.BlockSpec((n_ctx, hw), lambda b, h, i: (ctx_block0 + b, 2 * DA_HEADS + h)),
                  pl.BlockSpec((None, 1, hw), lambda b, h, i: (layer, 0, 0))],
        out_specs=pl.BlockSpec((tq, hw), q_index),
        out_shape=jax.ShapeDtypeStruct((M, W), BF16),
        compiler_params=_params(3),
    )(lam, qkv, qkv, qkv, qkv, qkv, subln_g.reshape(subln_g.shape[0], 1, hw))


def _router_kernel(x_ref, ada_ref, g_ref, wr_ref, br_ref, h_ref, ti_ref, tg_ref, *, stream, shift_col, scale_col):
    tm = x_ref.shape[0]
    n_exp = wr_ref.shape[1]
    row = stream.mod_row(pl.program_id(0) * tm)
    h = _modulated_norm(x_ref[...], g_ref[...], ada_ref, row, shift_col, scale_col)
    h_ref[...] = h.astype(BF16)
    logits = jnp.dot(h, wr_ref[...], precision=lax.Precision.HIGHEST, preferred_element_type=F32) + br_ref[...]

    lane = lax.broadcasted_iota(jnp.int32, logits.shape, 1)
    out_lane = lax.broadcasted_iota(jnp.int32, ti_ref.shape, 1)
    top_i = jnp.zeros(ti_ref.shape, jnp.int32)
    top_v = []
    vals = logits
    for k in range(TOP_K):
        m = jnp.max(vals, axis=-1, keepdims=True)
        sel = jnp.min(jnp.where(vals == m, lane, n_exp), axis=-1, keepdims=True)
        top_i = jnp.where(out_lane == k, sel, top_i)
        top_v.append(m)
        vals = jnp.where(lane == sel, -jnp.inf, vals)
    ti_ref[...] = top_i

    es = [jnp.exp(v - top_v[0]) for v in top_v]
    total = functools.reduce(jnp.add, es)
    gates = jnp.zeros(tg_ref.shape, F32)
    for k in range(TOP_K):
        gates = jnp.where(out_lane == k, es[k] / total, gates)
    tg_ref[...] = gates


def route(x, n_rows, ada, g, wr, br, layer, stream, *, shift_col, scale_col, tm=256):
    D = x.shape[1]
    E = wr.shape[2]
    tm = stream.row_tile(tm)
    assert n_rows % tm == 0
    kernel = functools.partial(_router_kernel, stream=stream, shift_col=shift_col, scale_col=scale_col)
    h, top_i, gates = pl.pallas_call(
        kernel,
        grid=(n_rows // tm,),
        in_specs=[pl.BlockSpec((tm, D), lambda i: (i, 0)),
                  pl.BlockSpec(ada.shape, lambda i: (0, 0)),
                  pl.BlockSpec((1, D), lambda i: (0, 0)),
                  pl.BlockSpec((None, D, E), lambda i: (layer, 0, 0)),
                  pl.BlockSpec((None, 1, E), lambda i: (layer, 0, 0))],
        out_specs=[pl.BlockSpec((tm, D), lambda i: (i, 0)),
                   pl.BlockSpec((tm, LANES), lambda i: (i, 0)),
                   pl.BlockSpec((tm, LANES), lambda i: (i, 0))],
        out_shape=[jax.ShapeDtypeStruct((n_rows, D), BF16),
                   jax.ShapeDtypeStruct((n_rows, LANES), jnp.int32),
                   jax.ShapeDtypeStruct((n_rows, LANES), F32)],
        compiler_params=_params(1),
    )(x, ada, g.reshape(1, D), wr, br.reshape(br.shape[0], 1, E))
    return h, top_i[:, :TOP_K], gates[:, :TOP_K]


def _moe_kernel(te_ref, nt_ref, x_ref, g_ref, w1_ref, b1_ref, w2_ref, b2_ref, o_ref, w1_scr, w2_scr, *, n_chunks):
    t = pl.program_id(0)
    ff = w2_ref.shape[0]
    fc = ff // n_chunks
    used = t < nt_ref[0]
    new_expert = jnp.logical_or(t == 0, te_ref[t] != te_ref[jnp.maximum(t - 1, 0)])

    @pl.when(jnp.logical_and(used, new_expert))
    def _():
        for c in range(2 * n_chunks):
            cols = slice(c * fc, (c + 1) * fc)
            w1_scr[:, cols] = w1_ref[:, cols].astype(BF16)
        for c in range(n_chunks):
            rows = slice(c * fc, (c + 1) * fc)
            w2_scr[rows, :] = w2_ref[rows, :].astype(BF16)

    @pl.when(used)
    def _():
        x = x_ref[...]
        y = jnp.zeros(o_ref.shape, F32)
        for c in range(n_chunks):
            glu_cols = slice(c * fc, (c + 1) * fc)
            lin_cols = slice(ff + c * fc, ff + (c + 1) * fc)
            glu = jnp.dot(x, w1_scr[:, glu_cols], preferred_element_type=F32) + b1_ref[:, glu_cols]
            lin = jnp.dot(x, w1_scr[:, lin_cols], preferred_element_type=F32) + b1_ref[:, lin_cols]
            glu = jnp.minimum(glu, SWIGLU_LIMIT)
            lin = jnp.clip(lin, -SWIGLU_LIMIT, SWIGLU_LIMIT)
            act = glu * jax.nn.sigmoid(SWIGLU_ALPHA * glu) * (lin + 1.0)
            y = y + jnp.dot(act.astype(BF16), w2_scr[glu_cols, :], preferred_element_type=F32)
        o_ref[...] = ((y + b2_ref[...]) * g_ref[...]).astype(o_ref.dtype)

    @pl.when(jnp.logical_not(used))
    def _():
        o_ref[...] = jnp.zeros(o_ref.shape, o_ref.dtype)


def moe_experts(xs, row_gate, tile_expert, n_tiles_used, w1, b1, w2, b2, layer, *, tm):
    P, D = xs.shape
    E, _, F2 = w1.shape[1:]
    ff = F2 // 2
    n_tiles = P // tm
    kernel = functools.partial(_moe_kernel, n_chunks=3)
    grid_spec = pltpu.PrefetchScalarGridSpec(
        num_scalar_prefetch=2,
        grid=(n_tiles,),
        in_specs=[pl.BlockSpec((tm, D), lambda t, te, nt: (t, 0)),
                  pl.BlockSpec((tm, 1), lambda t, te, nt: (t, 0)),
                  pl.BlockSpec((None, None, D, F2), lambda t, te, nt: (layer, te[t], 0, 0)),
                  pl.BlockSpec((None, None, 1, F2), lambda t, te, nt: (layer, te[t], 0, 0)),
                  pl.BlockSpec((None, None, ff, D), lambda t, te, nt: (layer, te[t], 0, 0)),
                  pl.BlockSpec((None, None, 1, D), lambda t, te, nt: (layer, te[t], 0, 0))],
        out_specs=pl.BlockSpec((tm, D), lambda t, te, nt: (t, 0)),
        scratch_shapes=[pltpu.VMEM((D, F2), BF16), pltpu.VMEM((ff, D), BF16)],
    )
    return pl.pallas_call(
        kernel,
        grid_spec=grid_spec,
        out_shape=jax.ShapeDtypeStruct((P, D), BF16),
        compiler_params=_params(1),
    )(tile_expert, n_tiles_used, xs, row_gate, w1, b1.reshape(b1.shape[0], E, 1, F2), w2,
      b2.reshape(b2.shape[0], E, 1, D))


def moe_ffn(h, top_i, gates, w1, b1, w2, b2, layer, *, tm=256):
    N, D = h.shape
    E = w1.shape[1]
    P = N * TOP_K
    n_tiles = P // tm + E
    i32 = jnp.int32

    pair_e = top_i.reshape(P)
    pair_ids = jnp.arange(P, dtype=i32)
    sorted_e, order = lax.sort((pair_e, pair_ids), num_keys=1, is_stable=True)
    _, rank = lax.sort((order, pair_ids), num_keys=1)
    bounds = jnp.searchsorted(sorted_e, jnp.arange(E + 1, dtype=i32), side="left").astype(i32)
    start_sorted, counts = bounds[:E], bounds[1:] - bounds[:E]
    tiles_per = (counts + tm - 1) // tm
    tile_end = jnp.cumsum(tiles_per)
    tile_start = tile_end - tiles_per
    n_used = tile_end[-1:]
    tile_ids = jnp.minimum(jnp.arange(n_tiles, dtype=i32), n_used - 1)
    tile_expert = jnp.sum((tile_ids[:, None] >= tile_end[None, :]).astype(i32), axis=1)

    row_e = jnp.repeat(tile_expert, tm)
    row = jnp.arange(n_tiles * tm, dtype=i32)
    off = row - tile_start[row_e] * tm
    valid = (off < counts[row_e]) & (row < n_used * tm)
    row_pair = order[jnp.clip(start_sorted[row_e] + off, 0, P - 1)]
    row_token = jnp.where(valid, row_pair // TOP_K, 0)
    row_gate = jnp.where(valid, gates.reshape(P)[row_pair], 0.0)
    pair_row = (tile_start[pair_e] * tm + rank - start_sorted[pair_e]).reshape(N, TOP_K)

    xs = h[row_token]
    ys = moe_experts(xs, row_gate[:, None], tile_expert, n_used, w1, b1, w2, b2, layer, tm=tm)
    return [ys[pair_row[:, k]] for k in range(TOP_K)]


def _combine_kernel(x_ref, gate_ref, *refs, stream):
    y_refs, o_ref = refs[:-1], refs[-1]
    tm = x_ref.shape[0]
    gate = gate_ref[pl.ds(stream.mod_row(pl.program_id(0) * tm), 1), :]
    y = functools.reduce(jnp.add, [y_ref[...].astype(F32) for y_ref in y_refs])
    o_ref[...] = x_ref[...] + gate * y


def combine_residual(x, ys, ada, gate_col, stream, *, tm=512):
    n, D = ys[0].shape
    tm = stream.row_tile(tm)
    assert n % tm == 0
    row_spec = pl.BlockSpec((tm, D), lambda i: (i, 0))
    return pl.pallas_call(
        functools.partial(_combine_kernel, stream=stream),
        grid=(n // tm,),
        in_specs=[row_spec, pl.BlockSpec((ADA_ROWS, D), lambda i: (0, gate_col))] + [row_spec] * len(ys),
        out_specs=row_spec,
        out_shape=jax.ShapeDtypeStruct((n, D), F32),
        compiler_params=_params(1),
    )(x, ada, *ys)


def rms_norm(x, g):
    return x * lax.rsqrt(jnp.mean(x * x, axis=-1, keepdims=True) + EPS) * g


def kernel(x, c, ctx, c_ctx, ada_w, ada_b, norm_g, final_g, ev_w_in, ev_w_out, hy_conv_w, hy_conv_b,
           hy_f_w1, hy_f_b1, hy_f_w2, hy_f_b2, hy_f_w3, hy_f_b3, hy_f_w4, hy_skip, gm_ln_g, gm_ws, gm_bs,
           od_w_qkv, od_w_out, od_lambda, od_subln_g, moe_wr, moe_br, moe_w1, moe_b1, moe_w2, moe_b2):
    B, L, D = x.shape
    n_ctx = ctx.shape[1]
    depth = ada_w.shape[0]
    hy_width = hy_skip.shape[2]
    hy_split = (HY_ORDER + 1) * hy_width
    gm_width = gm_ln_g.shape[1]
    stream = Stream(B, L, n_ctx)
    n_lat, M = stream.n_lat, stream.rows

    xs = jnp.concatenate([x.reshape(n_lat, D), ctx.reshape(B * n_ctx, D)], axis=0)
    cond = jnp.concatenate([jax.nn.silu(c), jax.nn.silu(c_ctx)[None], jnp.zeros((ADA_ROWS - B - 1, D), F32)], axis=0)
    ev_w_in, ev_w_out, od_w_qkv, od_w_out = (w.astype(BF16) for w in (ev_w_in, ev_w_out, od_w_qkv, od_w_out))

    for l in range(depth):
        i = l // 2
        last = l == depth - 1
        ada = matmul(cond.astype(BF16), ada_w, l, tm=ADA_ROWS, tn=1024) + ada_b[l]

        if l % 2 == 0:
            p = norm_matmul(xs, ada, norm_g[l, 0], ev_w_in, i, stream, shift_col=0, scale_col=1, out_dtype=BF16)
            filt = (hy_f_w1[i], hy_f_b1[i], hy_f_w2[i], hy_f_b2[i], hy_f_w3[i], hy_f_b3[i], hy_f_w4[i])
            y_a = jnp.concatenate(
                [hyena_mixer(p, 0, B, L, hy_width, hy_conv_w, hy_conv_b, filt, hy_skip, i),
                 hyena_mixer(p, n_lat, B, n_ctx, hy_width, hy_conv_w, hy_conv_b, filt, hy_skip, i)], axis=0)
            y_b = chunk_gmlp(p, hy_split, gm_width, gm_ln_g, gm_ws, gm_bs, i)
            xs = matmul_residual([y_a, y_b], ev_w_out, i, xs, ada, 2, stream)
        else:
            lam_init = 0.8 - 0.6 * math.exp(-0.3 * l)
            W = od_w_qkv.shape[2] // 3
            qkv = norm_matmul(xs, ada, norm_g[l, 0], od_w_qkv, i, stream, shift_col=0, scale_col=1, out_dtype=BF16,
                              rope_cols=(2 * W, W, math.log2(math.e) / math.sqrt(DA_HEAD_DIM)))
            lp = od_lambda[i]
            lam = (jnp.exp(jnp.sum(lp[0] * lp[1])) - jnp.exp(jnp.sum(lp[2] * lp[3])) + lam_init).reshape(1)
            o = diff_attention(qkv, lam, od_subln_g, i, 1.0 - lam_init, stream)
            xs = matmul_residual([o], od_w_out, i, xs, ada, 2, stream)

        n_tok = n_lat if last else M
        h, top_i, gates = route(xs, n_tok, ada, norm_g[l, 1], moe_wr, moe_br, l, stream, shift_col=3, scale_col=4)
        outs = moe_ffn(h, top_i, gates, moe_w1, moe_b1, moe_w2, moe_b2, l)
        xs = combine_residual(xs, outs, ada, 5, stream)

    return rms_norm(xs[:n_lat].reshape(B, L, D), final_g)
```

```python
import functools
import math

import jax
import jax.numpy as jnp
from jax import lax
from jax.experimental import pallas as pl
from jax.experimental.pallas import tpu as pltpu

F32 = jnp.float32
BF16 = jnp.bfloat16

EPS = 1e-6
GRID_W = 64

HY_ORDER = 2
HY_EMB = 33
HY_BANDS = (HY_EMB - 1) // 2
HY_DECAY_TARGET = 1e-2
HY_FAST_DECAY = 0.3
HY_SLOW_DECAY = 1.5

GM_GROUPS = 8
GM_CHUNK = 128

DA_HEADS = 8
DA_HEAD_DIM = 128
ROPE_AXIS_DIM = DA_HEAD_DIM // 2
ROPE_THETA = 10000.0

TOP_K = 4
SWIGLU_ALPHA = 1.702
SWIGLU_LIMIT = 7.0

LANES = 128
MIB = 1024 * 1024
VMEM_LIMIT_BYTES = 58 * MIB
ADA_ROWS = 8


def _params(n_grid_dims, vmem=VMEM_LIMIT_BYTES):
    return pltpu.CompilerParams(dimension_semantics=("arbitrary",) * n_grid_dims, vmem_limit_bytes=vmem)


def _tile(n, want, align=LANES):
    if n <= want:
        return n
    return next(t for t in range(want - want % align, 0, -align) if n % t == 0)


class Stream:
    def __init__(self, B, L, n_ctx):
        assert B + 1 <= ADA_ROWS
        self.B, self.L, self.n_ctx = B, L, n_ctx
        self.n_lat = B * L
        self.rows = B * L + B * n_ctx

    def row_tile(self, want):
        return _tile(math.gcd(self.L, self.B * self.n_ctx), want)

    def mod_row(self, row0):
        return jnp.where(row0 < self.n_lat, row0 // self.L, self.B)


def _mm_kernel(a_ref, w_ref, o_ref):
    o_ref[...] = jnp.dot(a_ref[...].astype(BF16), w_ref[...].astype(BF16),
                         preferred_element_type=F32).astype(o_ref.dtype)


def matmul(a, w, layer, *, tm=512, tn=512):
    M, K = a.shape
    N = w.shape[2]
    tm, tn = _tile(M, tm, 8), _tile(N, tn)
    return pl.pallas_call(
        _mm_kernel,
        grid=(N // tn, M // tm),
        in_specs=[pl.BlockSpec((tm, K), lambda j, i: (i, 0)),
                  pl.BlockSpec((None, K, tn), lambda j, i: (layer, 0, j))],
        out_specs=pl.BlockSpec((tm, tn), lambda j, i: (i, j)),
        out_shape=jax.ShapeDtypeStruct((M, N), F32),
        compiler_params=_params(2),
    )(a, w)


def _modulated_norm(x, g, ada_ref, row, shift_col, scale_col):
    D = x.shape[1]
    y = x * lax.rsqrt(jnp.mean(x * x, axis=-1, keepdims=True) + EPS) * g
    scale = ada_ref[pl.ds(row, 1), scale_col * D:(scale_col + 1) * D]
    shift = ada_ref[pl.ds(row, 1), shift_col * D:(shift_col + 1) * D]
    return y * (1.0 + scale) + shift


def _norm_mm_kernel(*refs, stream, shift_col, scale_col, rope):
    if rope is None:
        x_ref, ada_ref, g_ref, w_ref, o_ref, h_scr = refs
    else:
        x_ref, ada_ref, g_ref, w_ref, cos_ref, sin_ref, o_ref, h_scr = refs
    i, j = pl.program_id(0), pl.program_id(1)
    tm, tn = o_ref.shape

    @pl.when(j == 0)
    def _():
        h = _modulated_norm(x_ref[...], g_ref[...], ada_ref, stream.mod_row(i * tm), shift_col, scale_col)
        h_scr[...] = h.astype(BF16)

    y = jnp.dot(h_scr[...], w_ref[...], preferred_element_type=F32)
    if rope is None:
        o_ref[...] = y.astype(o_ref.dtype)
        return

    n_rope_tiles, n_q_tiles, q_scale = rope

    @pl.when(j < n_rope_tiles)
    def _():
        scale = jnp.where(j < n_q_tiles, q_scale, 1.0)
        lane = lax.broadcasted_iota(jnp.int32, (tm, DA_HEAD_DIM), 1)
        first_half = (lane % ROPE_AXIS_DIM) < ROPE_AXIS_DIM // 2
        for hh in range(tn // DA_HEAD_DIM):
            cols = slice(hh * DA_HEAD_DIM, (hh + 1) * DA_HEAD_DIM)
            seg = y[:, cols]
            partner = jnp.where(first_half,
                                pltpu.roll(seg, DA_HEAD_DIM - ROPE_AXIS_DIM // 2, 1),
                                pltpu.roll(seg, ROPE_AXIS_DIM // 2, 1))
            o_ref[:, cols] = ((seg * cos_ref[...] + partner * sin_ref[...]) * scale).astype(o_ref.dtype)

    @pl.when(j >= n_rope_tiles)
    def _():
        o_ref[...] = y.astype(o_ref.dtype)


def rope_tables(L, tile_rows):
    pos = jnp.arange(L)
    inv = ROPE_THETA ** (-jnp.arange(0, ROPE_AXIS_DIM, 2, dtype=F32) / ROPE_AXIS_DIM)
    ar = (pos // GRID_W).astype(F32)[:, None] * inv
    ac = (pos % GRID_W).astype(F32)[:, None] * inv
    cos = jnp.concatenate([jnp.cos(ar), jnp.cos(ar), jnp.cos(ac), jnp.cos(ac)], axis=-1)
    sin = jnp.concatenate([-jnp.sin(ar), jnp.sin(ar), -jnp.sin(ac), jnp.sin(ac)], axis=-1)
    cos = jnp.concatenate([cos, jnp.ones((tile_rows, DA_HEAD_DIM), F32)], axis=0)
    sin = jnp.concatenate([sin, jnp.zeros((tile_rows, DA_HEAD_DIM), F32)], axis=0)
    return cos, sin


def norm_matmul(x, ada, g, w, layer, stream, *, shift_col, scale_col, out_dtype, rope_cols=None, tm=1024, tn=512):
    M, D = x.shape
    N = w.shape[2]
    tm, tn = stream.row_tile(tm), _tile(N, tn)
    in_specs = [pl.BlockSpec((tm, D), lambda i, j: (i, 0)),
                pl.BlockSpec(ada.shape, lambda i, j: (0, 0)),
                pl.BlockSpec((1, D), lambda i, j: (0, 0)),
                pl.BlockSpec((None, D, tn), lambda i, j: (layer, 0, j))]
    args = [x, ada, g.reshape(1, D), w]
    rope = None
    if rope_cols is not None:
        n_rope, n_q, q_scale = rope_cols
        assert n_rope % tn == 0 and n_q % tn == 0 and tn % DA_HEAD_DIM == 0
        rope = (n_rope // tn, n_q // tn, q_scale)
        lat_tiles, tiles_per_sample = stream.n_lat // tm, stream.L // tm
        table_spec = pl.BlockSpec(
            (tm, DA_HEAD_DIM), lambda i, j: (jnp.where(i < lat_tiles, i % tiles_per_sample, tiles_per_sample), 0))
        in_specs += [table_spec, table_spec]
        args += list(rope_tables(stream.L, tm))
    kernel = functools.partial(_norm_mm_kernel, stream=stream, shift_col=shift_col, scale_col=scale_col, rope=rope)
    return pl.pallas_call(
        kernel,
        grid=(M // tm, N // tn),
        in_specs=in_specs,
        out_specs=pl.BlockSpec((tm, tn), lambda i, j: (i, j)),
        out_shape=jax.ShapeDtypeStruct((M, N), out_dtype),
        scratch_shapes=[pltpu.VMEM((tm, D), BF16)],
        compiler_params=_params(2),
    )(*args)


def _mm_res_kernel(*refs, n_a, stream):
    a_refs, w_refs = refs[:n_a], refs[n_a:2 * n_a]
    x_ref, gate_ref, o_ref = refs[2 * n_a:]
    tm = x_ref.shape[0]
    y = jnp.dot(a_refs[0][...], w_refs[0][...], preferred_element_type=F32)
    for a_ref, w_ref in zip(a_refs[1:], w_refs[1:]):
        y = y + jnp.dot(a_ref[...], w_ref[...], preferred_element_type=F32)
    gate = gate_ref[pl.ds(stream.mod_row(pl.program_id(0) * tm), 1), :]
    o_ref[...] = x_ref[...] + gate * y


def matmul_residual(a_list, w, layer, x, ada, gate_col, stream, *, tm=1024, tn=512):
    M, D = x.shape
    n_a = len(a_list)
    ka = a_list[0].shape[1]
    assert all(a.shape == (M, ka) for a in a_list) and w.shape[1:] == (n_a * ka, D)
    tm, tn = stream.row_tile(tm), _tile(D, tn)
    gate_block0 = gate_col * (D // tn)
    kernel = functools.partial(_mm_res_kernel, n_a=n_a, stream=stream)
    a_specs = [pl.BlockSpec((tm, ka), lambda i, j: (i, 0)) for _ in a_list]
    w_specs = [pl.BlockSpec((None, ka, tn), functools.partial(lambda i, j, r: (layer, r, j), r=r)) for r in range(n_a)]
    return pl.pallas_call(
        kernel,
        grid=(M // tm, D // tn),
        in_specs=a_specs + w_specs + [pl.BlockSpec((tm, tn), lambda i, j: (i, j)),
                                      pl.BlockSpec((ADA_ROWS, tn), lambda i, j: (0, gate_block0 + j))],
        out_specs=pl.BlockSpec((tm, tn), lambda i, j: (i, j)),
        out_shape=jax.ShapeDtypeStruct((M, D), F32),
        compiler_params=_params(2),
    )(*a_list, *([w] * n_a), x, ada)


def _lmm_kernel(m_ref, u_ref, o_ref):
    o_ref[...] = jnp.dot(m_ref[...], u_ref[...].astype(BF16), preferred_element_type=F32)


def left_matmul(mat, u, *, tm=512, tn=512):
    H, R, K = mat.shape
    B, _, C = u.shape
    tm, tn = _tile(R, tm), _tile(C, tn)
    tiles = R // tm
    return pl.pallas_call(
        _lmm_kernel,
        grid=(B, C // tn, H * tiles),
        in_specs=[pl.BlockSpec((None, tm, K), lambda b, j, i: (i // tiles, i % tiles, 0)),
                  pl.BlockSpec((None, K, tn), lambda b, j, i: (b, 0, j))],
        out_specs=pl.BlockSpec((None, tm, tn), lambda b, j, i: (b, i, j)),
        out_shape=jax.ShapeDtypeStruct((B, H * R, C), F32),
        compiler_params=_params(3),
    )(mat, u)


def dft_matrices(L):
    n = 2 * L
    f = lax.broadcasted_iota(jnp.int32, (L, L), 0)
    t = lax.broadcasted_iota(jnp.int32, (L, L), 1)
    ang = ((f * t) % n).astype(F32) * (2.0 * math.pi / n)
    re = jnp.cos(ang)
    im = jnp.where(f == 0, (1 - 2 * (t % 2)).astype(F32), -jnp.sin(ang))
    weight = jnp.where(jnp.arange(L) == 0, 1.0 / n, 2.0 / n)[None, :]
    fwd = jnp.stack([re, im]).astype(BF16)
    inv = jnp.stack([re * weight, im.T * weight]).astype(BF16)
    return fwd, inv


def hyena_filters(L, w1, b1, w2, b2, w3, b3, w4, width):
    hp = lax.Precision.HIGHEST
    t = jnp.linspace(0.0, 1.0, L, dtype=F32)[:, None]
    w = 2.0 * math.pi * jnp.arange(L, dtype=F32)[:, None] / L
    f = jnp.linspace(1e-4, HY_BANDS - 1, HY_BANDS, dtype=F32)[None]
    z = jnp.concatenate([t, jnp.cos(f * w), -jnp.sin(f * w)], axis=-1)
    h = jnp.sin(jnp.dot(z, w1, precision=hp) + b1)
    h = jnp.sin(jnp.dot(h, w2, precision=hp) + b2)
    h = jnp.sin(jnp.dot(h, w3, precision=hp) + b3)
    h = jnp.dot(h, w4, precision=hp).reshape(L, HY_ORDER, 2, width)
    max_decay = math.log(HY_DECAY_TARGET) / HY_FAST_DECAY
    min_decay = math.log(HY_DECAY_TARGET) / HY_SLOW_DECAY
    deltas = jnp.abs(jnp.linspace(min_decay, max_decay, HY_ORDER * width, dtype=F32)).reshape(HY_ORDER, 1, width)
    h = h * jnp.exp(-t[:, :, None, None] * deltas)
    fwd, bwd = h[:, :, 0], h[:, :, 1]
    bwd = bwd.at[0].set(0.0)
    scale = lax.rsqrt(jnp.sum(fwd * fwd, axis=0) + jnp.sum(bwd * bwd, axis=0) + EPS)
    return (fwd * scale).reshape(L, -1), (bwd * scale).reshape(L, -1)


def filter_spectrum(L, filt, width, dft_fwd):
    fwd, bwd = hyena_filters(L, *filt, width)
    spec = left_matmul(dft_fwd, jnp.stack([fwd, bwd]))
    r = lax.broadcasted_iota(jnp.int32, (2 * L, 1), 0)
    return (spec[0] + jnp.where(r <= L, 1.0, -1.0) * spec[1]).reshape(2, L, -1)


def _dft_mul_kernel(m_ref, u_ref, k_ref, o_ref):
    i = pl.program_id(2)
    tm = o_ref.shape[1]
    u = u_ref[...]
    za = jnp.dot(m_ref[0], u, preferred_element_type=F32)
    zb = jnp.dot(m_ref[1], u, preferred_element_type=F32)
    ka, kb = k_ref[0], k_ref[1]
    first = (i * tm + lax.broadcasted_iota(jnp.int32, za.shape, 0)) == 0
    bb = zb * kb
    o_ref[0] = (za * ka - jnp.where(first, 0.0, bb)).astype(o_ref.dtype)
    o_ref[1] = jnp.where(first, bb, za * kb + zb * ka).astype(o_ref.dtype)


def dft_multiply(dft_fwd, u, u_col0, k_spec, k_col0, n_seq, width, *, tm=512, tn=512):
    L = dft_fwd.shape[1]
    tm, tn = _tile(L, tm), _tile(width, tn)
    assert u_col0 % tn == 0 and k_col0 % tn == 0 and u.shape[0] == n_seq * L
    ucb, kcb = u_col0 // tn, k_col0 // tn
    return pl.pallas_call(
        _dft_mul_kernel,
        grid=(n_seq, width // tn, L // tm),
        in_specs=[pl.BlockSpec((2, tm, L), lambda b, j, i: (0, i, 0)),
                  pl.BlockSpec((L, tn), lambda b, j, i: (b, ucb + j)),
                  pl.BlockSpec((2, tm, tn), lambda b, j, i: (0, i, kcb + j))],
        out_specs=pl.BlockSpec((None, 2, tm, tn), lambda b, j, i: (b, 0, i, j)),
        out_shape=jax.ShapeDtypeStruct((n_seq, 2, L, width), BF16),
        compiler_params=_params(3),
    )(dft_fwd, u, k_spec)


def _idft_gate_kernel(m_ref, y_ref, u_ref, skip_ref, gate_ref, o_ref):
    y = (jnp.dot(m_ref[0], y_ref[0], preferred_element_type=F32)
         + jnp.dot(m_ref[1], y_ref[1], preferred_element_type=F32))
    y = y + u_ref[...].astype(F32) * skip_ref[...]
    o_ref[...] = (gate_ref[...].astype(F32) * y).astype(o_ref.dtype)


def idft_gate(dft_inv, y_spec, u, u_col0, skip, skip_row, gate, gate_col0, *, tm=512, tn=512):
    L = dft_inv.shape[1]
    n_seq, _, _, width = y_spec.shape
    tm, tn = _tile(L, tm), _tile(width, tn)
    assert u_col0 % tn == 0 and gate_col0 % tn == 0
    ucb, gcb, tiles = u_col0 // tn, gate_col0 // tn, L // tm
    return pl.pallas_call(
        _idft_gate_kernel,
        grid=(n_seq, width // tn, tiles),
        in_specs=[pl.BlockSpec((2, tm, L), lambda b, j, i: (0, i, 0)),
                  pl.BlockSpec((None, 2, L, tn), lambda b, j, i: (b, 0, 0, j)),
                  pl.BlockSpec((tm, tn), lambda b, j, i: (b * tiles + i, ucb + j)),
                  pl.BlockSpec((None, 1, tn), lambda b, j, i: (skip_row, 0, j)),
                  pl.BlockSpec((tm, tn), lambda b, j, i: (b * tiles + i, gcb + j))],
        out_specs=pl.BlockSpec((tm, tn), lambda b, j, i: (b * tiles + i, j)),
        out_shape=jax.ShapeDtypeStruct((n_seq * L, width), BF16),
        compiler_params=_params(3),
    )(dft_inv, y_spec, u, skip, gate)


def _short_conv_kernel(p_ref, w_ref, b_ref, o_ref):
    x = p_ref[...].astype(F32)
    n = x.shape[0]
    row = lax.broadcasted_iota(jnp.int32, x.shape, 0)
    prev = jnp.where(row == 0, 0.0, pltpu.roll(x, 1, 0))
    nxt = jnp.where(row == n - 1, 0.0, pltpu.roll(x, n - 1, 0))
    o_ref[...] = (prev * w_ref[0:1, :] + x * w_ref[1:2, :] + nxt * w_ref[2:3, :] + b_ref[...]).astype(o_ref.dtype)


def short_conv3(p, row0, n_seq, L, width, w, b, layer, *, tc=256):
    tc = _tile(width, tc)
    assert row0 % L == 0
    rb0 = row0 // L
    return pl.pallas_call(
        _short_conv_kernel,
        grid=(n_seq, width // tc),
        in_specs=[pl.BlockSpec((L, tc), lambda s, j: (rb0 + s, j)),
                  pl.BlockSpec((None, 3, tc), lambda s, j: (layer, 0, j)),
                  pl.BlockSpec((None, 1, tc), lambda s, j: (layer, 0, j))],
        out_specs=pl.BlockSpec((L, tc), lambda s, j: (s, j)),
        out_shape=jax.ShapeDtypeStruct((n_seq * L, width), BF16),
        compiler_params=_params(2),
    )(p, w, b.reshape(b.shape[0], 1, width))


def hyena_mixer(p, row0, n_seq, L, width, conv_w, conv_b, filt, skip, layer):
    vxx = short_conv3(p, row0, n_seq, L, (HY_ORDER + 1) * width, conv_w, conv_b, layer)
    dft_fwd, dft_inv = dft_matrices(L)
    k_spec = filter_spectrum(L, filt, width, dft_fwd)
    skip = skip.reshape(-1, 1, width)
    spec = dft_multiply(dft_fwd, vxx, 0, k_spec, 0, n_seq, width)
    z = idft_gate(dft_inv, spec, vxx, 0, skip, layer * HY_ORDER, vxx, width)
    spec = dft_multiply(dft_fwd, z, 0, k_spec, width, n_seq, width)
    return idft_gate(dft_inv, spec, z, 0, skip, layer * HY_ORDER + 1, vxx, 2 * width)


def _gmlp_kernel(u_ref, v_ref, g_ref, ws_ref, bs_ref, o_ref):
    v = v_ref[...].astype(F32)
    vc = v - jnp.mean(v, axis=-1, keepdims=True)
    vn = vc * lax.rsqrt(jnp.mean(vc * vc, axis=-1, keepdims=True) + EPS) * g_ref[...]
    vn = vn.astype(BF16)
    gd = v.shape[1] // GM_GROUPS
    for g in range(GM_GROUPS):
        cols = slice(g * gd, (g + 1) * gd)
        s = jnp.dot(ws_ref[g].astype(BF16), vn[:, cols], preferred_element_type=F32)
        o_ref[:, cols] = (u_ref[:, cols].astype(F32) * (s + bs_ref[:, cols])).astype(o_ref.dtype)


def chunk_gmlp(p, col0, width, ln_g, ws, bs, layer):
    M = p.shape[0]
    assert col0 % width == 0 and M % GM_CHUNK == 0
    cb = col0 // width
    bs_full = jnp.repeat(bs[layer].T, width // GM_GROUPS, axis=1)
    return pl.pallas_call(
        _gmlp_kernel,
        grid=(M // GM_CHUNK,),
        in_specs=[pl.BlockSpec((GM_CHUNK, width), lambda n: (n, cb)),
                  pl.BlockSpec((GM_CHUNK, width), lambda n: (n, cb + 1)),
                  pl.BlockSpec((None, 1, width), lambda n: (layer, 0, 0)),
                  pl.BlockSpec((None, GM_GROUPS, GM_CHUNK, GM_CHUNK), lambda n: (layer, 0, 0, 0)),
                  pl.BlockSpec((GM_CHUNK, width), lambda n: (0, 0))],
        out_specs=pl.BlockSpec((GM_CHUNK, width), lambda n: (n, 0)),
        out_shape=jax.ShapeDtypeStruct((M, width), BF16),
        compiler_params=_params(1),
    )(p, p, ln_g.reshape(ln_g.shape[0], 1, width), ws, bs_full)


def _attend(lam, q, kv_refs, g_ref, o_ref, out_scale):
    d = q.shape[1] // 2
    maps = []
    for m in range(2):
        cols = slice(m * d, (m + 1) * d)
        ss = [lax.dot_general(q[:, cols], k_ref[:, cols], (((1,), (1,)), ((), ())), preferred_element_type=F32)
              for k_ref, _ in kv_refs]
        mx = functools.reduce(jnp.maximum, [jnp.max(s, axis=-1, keepdims=True) for s in ss])
        es = [jnp.exp2(s - mx) for s in ss]
        total = functools.reduce(jnp.add, [jnp.sum(e, axis=-1, keepdims=True) for e in es])
        maps.append((es, total))
    (e0, l0), (e1, l1) = maps
    r0, r1 = 1.0 / l0, lam / l1
    o = None
    for piece, (_, v_ref) in enumerate(kv_refs):
        a = (e0[piece] * r0 - e1[piece] * r1).astype(BF16)
        pv = jnp.dot(a, v_ref[...], preferred_element_type=F32)
        o = pv if o is None else o + pv
    o = o * lax.rsqrt(jnp.mean(o * o, axis=-1, keepdims=True) + EPS)
    o_ref[...] = (o * (g_ref[...] * out_scale)).astype(o_ref.dtype)


def _attn_kernel(lam_ref, q_ref, kl_ref, vl_ref, kc_ref, vc_ref, g_ref, o_ref, *, n_lat_q, out_scale):
    i = pl.program_id(2)
    lam = lam_ref[0]

    @pl.when(i < n_lat_q)
    def _():
        _attend(lam, q_ref[...], [(kl_ref, vl_ref), (kc_ref, vc_ref)], g_ref, o_ref, out_scale)

    @pl.when(i >= n_lat_q)
    def _():
        _attend(lam, q_ref[...], [(kc_ref, vc_ref)], g_ref, o_ref, out_scale)


def diff_attention(qkv, lam, subln_g, layer, out_scale, stream, *, tq=256):
    M = qkv.shape[0]
    B, L, n_ctx = stream.B, stream.L, stream.n_ctx
    hw = 2 * DA_HEAD_DIM
    W = DA_HEADS * hw
    assert qkv.shape[1] == 3 * W
    tq = _tile(math.gcd(L, n_ctx), tq, 8)
    n_lat_q, n_ctx_q = L // tq, n_ctx // tq
    assert (B * L) % n_ctx == 0
    ctx_block0 = (B * L) // n_ctx

    def q_index(b, h, i):
        return (jnp.where(i < n_lat_q, b * n_lat_q + i, B * n_lat_q + b * n_ctx_q + (i - n_lat_q)), h)

    kernel = functools.partial(_attn_kernel, n_lat_q=n_lat_q, out_scale=out_scale)
    return pl.pallas_call(
        kernel,
        grid=(B, DA_HEADS, n_lat_q + n_ctx_q),
        in_specs=[pl.BlockSpec(memory_space=pltpu.SMEM),
                  pl.BlockSpec((tq, hw), q_index),
                  pl.BlockSpec((L, hw), lambda b, h, i: (b, DA_HEADS + h)),
                  pl.BlockSpec((L, hw), lambda b, h, i: (b, 2 * DA_HEADS + h)),
                  pl.BlockSpec((n_ctx, hw), lambda b, h, i: (ctx_block0 + b, DA_HEADS + h)),
                  pl.BlockSpec((n_ctx, hw), lambda b, h, i: (ctx_block0 + b, 2 * DA_HEADS + h)),
                  pl.BlockSpec((None, 1, hw), lambda b, h, i: (layer, 0, 0))],
        out_specs=pl.BlockSpec((tq, hw), q_index),
        out_shape=jax.ShapeDtypeStruct((M, W), BF16),
        compiler_params=_params(3),
    )(lam, qkv, qkv, qkv, qkv, qkv, subln_g.reshape(subln_g.shape[0], 1, hw))


def _router_kernel(x_ref, ada_ref, g_ref, wr_ref, br_ref, h_ref, ti_ref, tg_ref, *, stream, shift_col, scale_col):
    tm = x_ref.shape[0]
    n_exp = wr_ref.shape[1]
    row = stream.mod_row(pl.program_id(0) * tm)
    h = _modulated_norm(x_ref[...], g_ref[...], ada_ref, row, shift_col, scale_col)
    half = h.shape[1] // 2
    bits = pltpu.bitcast(h.astype(BF16).astype(F32), jnp.uint32)
    h_ref[...] = (bits[:, :half] >> 16) | (bits[:, half:] & jnp.uint32(0xFFFF0000))
    logits = jnp.dot(h, wr_ref[...], precision=lax.Precision.HIGHEST, preferred_element_type=F32) + br_ref[...]

    lane = lax.broadcasted_iota(jnp.int32, logits.shape, 1)
    out_lane = lax.broadcasted_iota(jnp.int32, ti_ref.shape, 1)
    top_i = jnp.zeros(ti_ref.shape, jnp.int32)
    top_v = []
    vals = logits
    for k in range(TOP_K):
        m = jnp.max(vals, axis=-1, keepdims=True)
        sel = jnp.min(jnp.where(vals == m, lane, n_exp), axis=-1, keepdims=True)
        top_i = jnp.where(out_lane == k, sel, top_i)
        top_v.append(m)
        vals = jnp.where(lane == sel, -jnp.inf, vals)
    ti_ref[...] = top_i

    es = [jnp.exp(v - top_v[0]) for v in top_v]
    total = functools.reduce(jnp.add, es)
    gates = jnp.zeros(tg_ref.shape, F32)
    for k in range(TOP_K):
        gates = jnp.where(out_lane == k, es[k] / total, gates)
    tg_ref[...] = gates


def route(x, n_rows, ada, g, wr, br, layer, stream, *, shift_col, scale_col, tm=256):
    D = x.shape[1]
    E = wr.shape[2]
    tm = stream.row_tile(tm)
    assert n_rows % tm == 0
    kernel = functools.partial(_router_kernel, stream=stream, shift_col=shift_col, scale_col=scale_col)
    h, top_i, gates = pl.pallas_call(
        kernel,
        grid=(n_rows // tm,),
        in_specs=[pl.BlockSpec((tm, D), lambda i: (i, 0)),
                  pl.BlockSpec(ada.shape, lambda i: (0, 0)),
                  pl.BlockSpec((1, D), lambda i: (0, 0)),
                  pl.BlockSpec((None, D, E), lambda i: (layer, 0, 0)),
                  pl.BlockSpec((None, 1, E), lambda i: (layer, 0, 0))],
        out_specs=[pl.BlockSpec((tm, D // 2), lambda i: (i, 0)),
                   pl.BlockSpec((tm, LANES), lambda i: (i, 0)),
                   pl.BlockSpec((tm, LANES), lambda i: (i, 0))],
        out_shape=[jax.ShapeDtypeStruct((n_rows, D // 2), jnp.uint32),
                   jax.ShapeDtypeStruct((n_rows, LANES), jnp.int32),
                   jax.ShapeDtypeStruct((n_rows, LANES), F32)],
        compiler_params=_params(1),
    )(x, ada, g.reshape(1, D), wr, br.reshape(br.shape[0], 1, E))
    return h, top_i[:, :TOP_K], gates[:, :TOP_K]


def _moe_kernel(te_ref, nt_ref, tok_ref, h_ref, g_ref, w1_ref, b1_ref, w2_ref, b2_ref, o_ref,
                x_buf, x_sem, w1_scr, w2_scr, *, n_chunks, n_tiles):
    t = pl.program_id(0)
    tm = o_ref.shape[0]
    ff = w2_ref.shape[0]
    fc = ff // n_chunks
    slot = t % 2

    def start_gather(tile, dst_slot):
        for r in range(tm):
            tok = tok_ref[tile * tm + r]
            pltpu.make_async_copy(h_ref.at[tok], x_buf.at[dst_slot, r], x_sem.at[dst_slot]).start()

    def wait_gather(dst_slot):
        pltpu.make_async_copy(h_ref.at[pl.ds(0, tm)], x_buf.at[dst_slot], x_sem.at[dst_slot]).wait()

    @pl.when(t == 0)
    def _():
        start_gather(0, 0)

    new_expert = jnp.logical_or(t == 0, te_ref[t] != te_ref[jnp.maximum(t - 1, 0)])

    used = t < nt_ref[0]

    @pl.when(jnp.logical_and(used, new_expert))
    def _():
        for c in range(2 * n_chunks):
            cols = slice(c * fc, (c + 1) * fc)
            w1_scr[:, cols] = w1_ref[:, cols].astype(BF16)
        for c in range(n_chunks):
            rows = slice(c * fc, (c + 1) * fc)
            w2_scr[rows, :] = w2_ref[rows, :].astype(BF16)

    wait_gather(slot)
    start_gather(jnp.minimum(t + 1, n_tiles - 1), 1 - slot)

    @pl.when(used)
    def _():
        words = jnp.concatenate([x_buf[slot, :, s, :] for s in range(x_buf.shape[2])], axis=-1)
        half = words.shape[1]
        x_lo = pltpu.bitcast(words << 16, F32).astype(BF16)
        x_hi = pltpu.bitcast(words & jnp.uint32(0xFFFF0000), F32).astype(BF16)

        def x_dot(w_cols):
            return (jnp.dot(x_lo, w1_scr[:half, w_cols], preferred_element_type=F32)
                    + jnp.dot(x_hi, w1_scr[half:, w_cols], preferred_element_type=F32))

        y = jnp.zeros(o_ref.shape, F32)
        for c in range(n_chunks):
            glu_cols = slice(c * fc, (c + 1) * fc)
            lin_cols = slice(ff + c * fc, ff + (c + 1) * fc)
            glu = x_dot(glu_cols) + b1_ref[:, glu_cols]
            lin = x_dot(lin_cols) + b1_ref[:, lin_cols]
            glu = jnp.minimum(glu, SWIGLU_LIMIT)
            lin = jnp.clip(lin, -SWIGLU_LIMIT, SWIGLU_LIMIT)
            act = glu * jax.nn.sigmoid(SWIGLU_ALPHA * glu) * (lin + 1.0)
            y = y + jnp.dot(act.astype(BF16), w2_scr[glu_cols, :], preferred_element_type=F32)
        o_ref[...] = ((y + b2_ref[...]) * g_ref[...]).astype(o_ref.dtype)

    @pl.when(jnp.logical_not(used))
    def _():
        o_ref[...] = jnp.zeros(o_ref.shape, o_ref.dtype)

    @pl.when(t == n_tiles - 1)
    def _():
        wait_gather(1 - slot)


def moe_experts(h, row_token, row_gate, tile_expert, n_tiles_used, w1, b1, w2, b2, layer, *, tm):
    N = h.shape[0]
    D = w1.shape[2]
    P = row_token.shape[0]
    assert h.shape[1] * 2 == D and h.shape[1] % LANES == 0
    h = h.reshape(N, h.shape[1] // LANES, LANES)
    E, _, F2 = w1.shape[1:]
    ff = F2 // 2
    n_tiles = P // tm
    kernel = functools.partial(_moe_kernel, n_chunks=3, n_tiles=n_tiles)
    grid_spec = pltpu.PrefetchScalarGridSpec(
        num_scalar_prefetch=3,
        grid=(n_tiles,),
        in_specs=[pl.BlockSpec(memory_space=pl.ANY),
                  pl.BlockSpec((tm, 1), lambda t, te, nt, tok: (t, 0)),
                  pl.BlockSpec((None, None, D, F2), lambda t, te, nt, tok: (layer, te[t], 0, 0)),
                  pl.BlockSpec((None, None, 1, F2), lambda t, te, nt, tok: (layer, te[t], 0, 0)),
                  pl.BlockSpec((None, None, ff, D), lambda t, te, nt, tok: (layer, te[t], 0, 0)),
                  pl.BlockSpec((None, None, 1, D), lambda t, te, nt, tok: (layer, te[t], 0, 0))],
        out_specs=pl.BlockSpec((tm, D), lambda t, te, nt, tok: (t, 0)),
        scratch_shapes=[pltpu.VMEM((2, tm) + h.shape[1:], jnp.uint32), pltpu.SemaphoreType.DMA((2,)),
                        pltpu.VMEM((D, F2), BF16), pltpu.VMEM((ff, D), BF16)],
    )
    return pl.pallas_call(
        kernel,
        grid_spec=grid_spec,
        out_shape=jax.ShapeDtypeStruct((P, D), BF16),
        compiler_params=_params(1),
    )(tile_expert, n_tiles_used, row_token, h, row_gate, w1, b1.reshape(b1.shape[0], E, 1, F2), w2,
      b2.reshape(b2.shape[0], E, 1, D))


def moe_ffn(h, top_i, gates, w1, b1, w2, b2, layer, *, tm=256):
    N = h.shape[0]
    E = w1.shape[1]
    P = N * TOP_K
    n_tiles = P // tm + E
    i32 = jnp.int32

    pair_e = top_i.reshape(P)
    pair_ids = jnp.arange(P, dtype=i32)
    sorted_e, order = lax.sort((pair_e, pair_ids), num_keys=1, is_stable=True)
    _, rank = lax.sort((order, pair_ids), num_keys=1)
    bounds = jnp.searchsorted(sorted_e, jnp.arange(E + 1, dtype=i32), side="left").astype(i32)
    start_sorted, counts = bounds[:E], bounds[1:] - bounds[:E]
    tiles_per = (counts + tm - 1) // tm
    tile_end = jnp.cumsum(tiles_per)
    tile_start = tile_end - tiles_per
    n_used = tile_end[-1:]
    tile_ids = jnp.minimum(jnp.arange(n_tiles, dtype=i32), n_used - 1)
    tile_expert = jnp.sum((tile_ids[:, None] >= tile_end[None, :]).astype(i32), axis=1)

    all_tiles = jnp.arange(n_tiles, dtype=i32)
    tile_off0 = (all_tiles - tile_start[tile_expert]) * tm
    off = tile_off0[:, None] + jnp.arange(tm, dtype=i32)[None, :]
    valid = (off < counts[tile_expert][:, None]) & (all_tiles < n_used)[:, None]
    sorted_pos = jnp.clip(start_sorted[tile_expert][:, None] + off, 0, P - 1)
    row_pair = order[sorted_pos.reshape(-1)]
    valid = valid.reshape(-1)
    row_token = jnp.where(valid, row_pair // TOP_K, 0)
    row_gate = jnp.where(valid, gates.reshape(P)[row_pair], 0.0)
    pair_row = (tile_start[pair_e] * tm + rank - start_sorted[pair_e]).reshape(N, TOP_K)

    ys = moe_experts(h, row_token, row_gate[:, None], tile_expert, n_used, w1, b1, w2, b2, layer, tm=tm)
    return [ys[pair_row[:, k]] for k in range(TOP_K)]


def _combine_kernel(x_ref, gate_ref, *refs, stream):
    y_refs, o_ref = refs[:-1], refs[-1]
    tm = x_ref.shape[0]
    gate = gate_ref[pl.ds(stream.mod_row(pl.program_id(0) * tm), 1), :]
    y = functools.reduce(jnp.add, [y_ref[...].astype(F32) for y_ref in y_refs])
    o_ref[...] = x_ref[...] + gate * y


def combine_residual(x, ys, ada, gate_col, stream, *, tm=512):
    n, D = ys[0].shape
    tm = stream.row_tile(tm)
    assert n % tm == 0
    row_spec = pl.BlockSpec((tm, D), lambda i: (i, 0))
    return pl.pallas_call(
        functools.partial(_combine_kernel, stream=stream),
        grid=(n // tm,),
        in_specs=[row_spec, pl.BlockSpec((ADA_ROWS, D), lambda i: (0, gate_col))] + [row_spec] * len(ys),
        out_specs=row_spec,
        out_shape=jax.ShapeDtypeStruct((n, D), F32),
        compiler_params=_params(1),
    )(x, ada, *ys)


def rms_norm(x, g):
    return x * lax.rsqrt(jnp.mean(x * x, axis=-1, keepdims=True) + EPS) * g


def kernel(x, c, ctx, c_ctx, ada_w, ada_b, norm_g, final_g, ev_w_in, ev_w_out, hy_conv_w, hy_conv_b,
           hy_f_w1, hy_f_b1, hy_f_w2, hy_f_b2, hy_f_w3, hy_f_b3, hy_f_w4, hy_skip, gm_ln_g, gm_ws, gm_bs,
           od_w_qkv, od_w_out, od_lambda, od_subln_g, moe_wr, moe_br, moe_w1, moe_b1, moe_w2, moe_b2):
    B, L, D = x.shape
    n_ctx = ctx.shape[1]
    depth = ada_w.shape[0]
    hy_width = hy_skip.shape[2]
    hy_split = (HY_ORDER + 1) * hy_width
    gm_width = gm_ln_g.shape[1]
    stream = Stream(B, L, n_ctx)
    n_lat, M = stream.n_lat, stream.rows

    xs = jnp.concatenate([x.reshape(n_lat, D), ctx.reshape(B * n_ctx, D)], axis=0)
    cond = jnp.concatenate([jax.nn.silu(c), jax.nn.silu(c_ctx)[None], jnp.zeros((ADA_ROWS - B - 1, D), F32)], axis=0)
    ev_w_in, ev_w_out, od_w_qkv, od_w_out = (w.astype(BF16) for w in (ev_w_in, ev_w_out, od_w_qkv, od_w_out))

    for l in range(depth):
        i = l // 2
        last = l == depth - 1
        ada = matmul(cond.astype(BF16), ada_w, l, tm=ADA_ROWS, tn=1024) + ada_b[l]

        if l % 2 == 0:
            p = norm_matmul(xs, ada, norm_g[l, 0], ev_w_in, i, stream, shift_col=0, scale_col=1, out_dtype=BF16)
            filt = (hy_f_w1[i], hy_f_b1[i], hy_f_w2[i], hy_f_b2[i], hy_f_w3[i], hy_f_b3[i], hy_f_w4[i])
            y_a = jnp.concatenate(
                [hyena_mixer(p, 0, B, L, hy_width, hy_conv_w, hy_conv_b, filt, hy_skip, i),
                 hyena_mixer(p, n_lat, B, n_ctx, hy_width, hy_conv_w, hy_conv_b, filt, hy_skip, i)], axis=0)
            y_b = chunk_gmlp(p, hy_split, gm_width, gm_ln_g, gm_ws, gm_bs, i)
            xs = matmul_residual([y_a, y_b], ev_w_out, i, xs, ada, 2, stream)
        else:
            lam_init = 0.8 - 0.6 * math.exp(-0.3 * l)
            W = od_w_qkv.shape[2] // 3
            qkv = norm_matmul(xs, ada, norm_g[l, 0], od_w_qkv, i, stream, shift_col=0, scale_col=1, out_dtype=BF16,
                              rope_cols=(2 * W, W, math.log2(math.e) / math.sqrt(DA_HEAD_DIM)))
            lp = od_lambda[i]
            lam = (jnp.exp(jnp.sum(lp[0] * lp[1])) - jnp.exp(jnp.sum(lp[2] * lp[3])) + lam_init).reshape(1)
            o = diff_attention(qkv, lam, od_subln_g, i, 1.0 - lam_init, stream)
            xs = matmul_residual([o], od_w_out, i, xs, ada, 2, stream)

        n_tok = n_lat if last else M
        h, top_i, gates = route(xs, n_tok, ada, norm_g[l, 1], moe_wr, moe_br, l, stream, shift_col=3, scale_col=4)
        outs = moe_ffn(h, top_i, gates, moe_w1, moe_b1, moe_w2, moe_b2, l)
        xs = combine_residual(xs, outs, ada, 5, stream)

    return rms_norm(xs[:n_lat].reshape(B, L, D), final_g)
```

```python
import functools
import math

import jax
import jax.numpy as jnp
from jax import lax
from jax.experimental import pallas as pl
from jax.experimental.pallas import tpu as pltpu

F32 = jnp.float32
BF16 = jnp.bfloat16

EPS = 1e-6
GRID_W = 64

HY_ORDER = 2
HY_EMB = 33
HY_BANDS = (HY_EMB - 1) // 2
HY_DECAY_TARGET = 1e-2
HY_FAST_DECAY = 0.3
HY_SLOW_DECAY = 1.5

GM_GROUPS = 8
GM_CHUNK = 128

DA_HEADS = 8
DA_HEAD_DIM = 128
ROPE_AXIS_DIM = DA_HEAD_DIM // 2
ROPE_THETA = 10000.0

TOP_K = 4
SWIGLU_ALPHA = 1.702
SWIGLU_LIMIT = 7.0

LANES = 128
MIB = 1024 * 1024
VMEM_LIMIT_BYTES = 58 * MIB
ADA_ROWS = 8


def _params(n_grid_dims, vmem=VMEM_LIMIT_BYTES):
    return pltpu.CompilerParams(dimension_semantics=("arbitrary",) * n_grid_dims, vmem_limit_bytes=vmem)


def _tile(n, want, align=LANES):
    if n <= want:
        return n
    return next(t for t in range(want - want % align, 0, -align) if n % t == 0)


class Stream:
    def __init__(self, B, L, n_ctx):
        assert B + 1 <= ADA_ROWS
        self.B, self.L, self.n_ctx = B, L, n_ctx
        self.n_lat = B * L
        self.rows = B * L + B * n_ctx

    def row_tile(self, want):
        return _tile(math.gcd(self.L, self.B * self.n_ctx), want)

    def mod_row(self, row0):
        return jnp.where(row0 < self.n_lat, row0 // self.L, self.B)


def _mm_kernel(a_ref, w_ref, o_ref):
    o_ref[...] = jnp.dot(a_ref[...].astype(BF16), w_ref[...].astype(BF16),
                         preferred_element_type=F32).astype(o_ref.dtype)


def matmul(a, w, layer, *, tm=512, tn=512):
    M, K = a.shape
    N = w.shape[2]
    tm, tn = _tile(M, tm, 8), _tile(N, tn)
    return pl.pallas_call(
        _mm_kernel,
        grid=(N // tn, M // tm),
        in_specs=[pl.BlockSpec((tm, K), lambda j, i: (i, 0)),
                  pl.BlockSpec((None, K, tn), lambda j, i: (layer, 0, j))],
        out_specs=pl.BlockSpec((tm, tn), lambda j, i: (i, j)),
        out_shape=jax.ShapeDtypeStruct((M, N), F32),
        compiler_params=_params(2),
    )(a, w)


def _modulated_norm(x, g, ada_ref, row, shift_col, scale_col):
    D = x.shape[1]
    y = x * lax.rsqrt(jnp.mean(x * x, axis=-1, keepdims=True) + EPS) * g
    scale = ada_ref[pl.ds(row, 1), scale_col * D:(scale_col + 1) * D]
    shift = ada_ref[pl.ds(row, 1), shift_col * D:(shift_col + 1) * D]
    return y * (1.0 + scale) + shift


def _norm_mm_kernel(*refs, stream, shift_col, scale_col, rope):
    if rope is None:
        x_ref, ada_ref, g_ref, w_ref, o_ref, h_scr = refs
    else:
        x_ref, ada_ref, g_ref, w_ref, cos_ref, sin_ref, o_ref, h_scr = refs
    i, j = pl.program_id(0), pl.program_id(1)
    tm, tn = o_ref.shape

    @pl.when(j == 0)
    def _():
        h = _modulated_norm(x_ref[...], g_ref[...], ada_ref, stream.mod_row(i * tm), shift_col, scale_col)
        h_scr[...] = h.astype(BF16)

    y = jnp.dot(h_scr[...], w_ref[...], preferred_element_type=F32)
    if rope is None:
        o_ref[...] = y.astype(o_ref.dtype)
        return

    n_rope_tiles, n_q_tiles, q_scale = rope

    @pl.when(j < n_rope_tiles)
    def _():
        scale = jnp.where(j < n_q_tiles, q_scale, 1.0)
        lane = lax.broadcasted_iota(jnp.int32, (tm, DA_HEAD_DIM), 1)
        first_half = (lane % ROPE_AXIS_DIM) < ROPE_AXIS_DIM // 2
        for hh in range(tn // DA_HEAD_DIM):
            cols = slice(hh * DA_HEAD_DIM, (hh + 1) * DA_HEAD_DIM)
            seg = y[:, cols]
            partner = jnp.where(first_half,
                                pltpu.roll(seg, DA_HEAD_DIM - ROPE_AXIS_DIM // 2, 1),
                                pltpu.roll(seg, ROPE_AXIS_DIM // 2, 1))
            o_ref[:, cols] = ((seg * cos_ref[...] + partner * sin_ref[...]) * scale).astype(o_ref.dtype)

    @pl.when(j >= n_rope_tiles)
    def _():
        o_ref[...] = y.astype(o_ref.dtype)


def rope_tables(L, tile_rows):
    pos = jnp.arange(L)
    inv = ROPE_THETA ** (-jnp.arange(0, ROPE_AXIS_DIM, 2, dtype=F32) / ROPE_AXIS_DIM)
    ar = (pos // GRID_W).astype(F32)[:, None] * inv
    ac = (pos % GRID_W).astype(F32)[:, None] * inv
    cos = jnp.concatenate([jnp.cos(ar), jnp.cos(ar), jnp.cos(ac), jnp.cos(ac)], axis=-1)
    sin = jnp.concatenate([-jnp.sin(ar), jnp.sin(ar), -jnp.sin(ac), jnp.sin(ac)], axis=-1)
    cos = jnp.concatenate([cos, jnp.ones((tile_rows, DA_HEAD_DIM), F32)], axis=0)
    sin = jnp.concatenate([sin, jnp.zeros((tile_rows, DA_HEAD_DIM), F32)], axis=0)
    return cos, sin


def norm_matmul(x, ada, g, w, layer, stream, *, shift_col, scale_col, out_dtype, rope_cols=None, tm=1024, tn=512):
    M, D = x.shape
    N = w.shape[2]
    tm, tn = stream.row_tile(tm), _tile(N, tn)
    in_specs = [pl.BlockSpec((tm, D), lambda i, j: (i, 0)),
                pl.BlockSpec(ada.shape, lambda i, j: (0, 0)),
                pl.BlockSpec((1, D), lambda i, j: (0, 0)),
                pl.BlockSpec((None, D, tn), lambda i, j: (layer, 0, j))]
    args = [x, ada, g.reshape(1, D), w]
    rope = None
    if rope_cols is not None:
        n_rope, n_q, q_scale = rope_cols
        assert n_rope % tn == 0 and n_q % tn == 0 and tn % DA_HEAD_DIM == 0
        rope = (n_rope // tn, n_q // tn, q_scale)
        lat_tiles, tiles_per_sample = stream.n_lat // tm, stream.L // tm
        table_spec = pl.BlockSpec(
            (tm, DA_HEAD_DIM), lambda i, j: (jnp.where(i < lat_tiles, i % tiles_per_sample, tiles_per_sample), 0))
        in_specs += [table_spec, table_spec]
        args += list(rope_tables(stream.L, tm))
    kernel = functools.partial(_norm_mm_kernel, stream=stream, shift_col=shift_col, scale_col=scale_col, rope=rope)
    return pl.pallas_call(
        kernel,
        grid=(M // tm, N // tn),
        in_specs=in_specs,
        out_specs=pl.BlockSpec((tm, tn), lambda i, j: (i, j)),
        out_shape=jax.ShapeDtypeStruct((M, N), out_dtype),
        scratch_shapes=[pltpu.VMEM((tm, D), BF16)],
        compiler_params=_params(2),
    )(*args)


def _mm_res_kernel(*refs, n_a, stream):
    a_refs, w_refs = refs[:n_a], refs[n_a:2 * n_a]
    x_ref, gate_ref, o_ref = refs[2 * n_a:]
    tm = x_ref.shape[0]
    y = jnp.dot(a_refs[0][...], w_refs[0][...], preferred_element_type=F32)
    for a_ref, w_ref in zip(a_refs[1:], w_refs[1:]):
        y = y + jnp.dot(a_ref[...], w_ref[...], preferred_element_type=F32)
    gate = gate_ref[pl.ds(stream.mod_row(pl.program_id(0) * tm), 1), :]
    o_ref[...] = x_ref[...] + gate * y


def matmul_residual(a_list, w, layer, x, ada, gate_col, stream, *, tm=1024, tn=512):
    M, D = x.shape
    n_a = len(a_list)
    ka = a_list[0].shape[1]
    assert all(a.shape == (M, ka) for a in a_list) and w.shape[1:] == (n_a * ka, D)
    tm, tn = stream.row_tile(tm), _tile(D, tn)
    gate_block0 = gate_col * (D // tn)
    kernel = functools.partial(_mm_res_kernel, n_a=n_a, stream=stream)
    a_specs = [pl.BlockSpec((tm, ka), lambda i, j: (i, 0)) for _ in a_list]
    w_specs = [pl.BlockSpec((None, ka, tn), functools.partial(lambda i, j, r: (layer, r, j), r=r)) for r in range(n_a)]
    return pl.pallas_call(
        kernel,
        grid=(M // tm, D // tn),
        in_specs=a_specs + w_specs + [pl.BlockSpec((tm, tn), lambda i, j: (i, j)),
                                      pl.BlockSpec((ADA_ROWS, tn), lambda i, j: (0, gate_block0 + j))],
        out_specs=pl.BlockSpec((tm, tn), lambda i, j: (i, j)),
        out_shape=jax.ShapeDtypeStruct((M, D), F32),
        compiler_params=_params(2),
    )(*a_list, *([w] * n_a), x, ada)


def _lmm_kernel(m_ref, u_ref, o_ref):
    o_ref[...] = jnp.dot(m_ref[...], u_ref[...].astype(BF16), preferred_element_type=F32)


def left_matmul(mat, u, *, tm=512, tn=512):
    H, R, K = mat.shape
    B, _, C = u.shape
    tm, tn = _tile(R, tm), _tile(C, tn)
    tiles = R // tm
    return pl.pallas_call(
        _lmm_kernel,
        grid=(B, C // tn, H * tiles),
        in_specs=[pl.BlockSpec((None, tm, K), lambda b, j, i: (i // tiles, i % tiles, 0)),
                  pl.BlockSpec((None, K, tn), lambda b, j, i: (b, 0, j))],
        out_specs=pl.BlockSpec((None, tm, tn), lambda b, j, i: (b, i, j)),
        out_shape=jax.ShapeDtypeStruct((B, H * R, C), F32),
        compiler_params=_params(3),
    )(mat, u)


def dft_matrices(L):
    n = 2 * L
    f = lax.broadcasted_iota(jnp.int32, (L, L), 0)
    t = lax.broadcasted_iota(jnp.int32, (L, L), 1)
    ang = ((f * t) % n).astype(F32) * (2.0 * math.pi / n)
    re = jnp.cos(ang)
    im = jnp.where(f == 0, (1 - 2 * (t % 2)).astype(F32), -jnp.sin(ang))
    weight = jnp.where(jnp.arange(L) == 0, 1.0 / n, 2.0 / n)[None, :]
    fwd = jnp.stack([re, im]).astype(BF16)
    inv = jnp.stack([re * weight, im.T * weight]).astype(BF16)
    return fwd, inv


def hyena_filters(L, w1, b1, w2, b2, w3, b3, w4, width):
    hp = lax.Precision.HIGHEST
    t = jnp.linspace(0.0, 1.0, L, dtype=F32)[:, None]
    w = 2.0 * math.pi * jnp.arange(L, dtype=F32)[:, None] / L
    f = jnp.linspace(1e-4, HY_BANDS - 1, HY_BANDS, dtype=F32)[None]
    z = jnp.concatenate([t, jnp.cos(f * w), -jnp.sin(f * w)], axis=-1)
    h = jnp.sin(jnp.dot(z, w1, precision=hp) + b1)
    h = jnp.sin(jnp.dot(h, w2, precision=hp) + b2)
    h = jnp.sin(jnp.dot(h, w3, precision=hp) + b3)
    h = jnp.dot(h, w4, precision=hp).reshape(L, HY_ORDER, 2, width)
    max_decay = math.log(HY_DECAY_TARGET) / HY_FAST_DECAY
    min_decay = math.log(HY_DECAY_TARGET) / HY_SLOW_DECAY
    deltas = jnp.abs(jnp.linspace(min_decay, max_decay, HY_ORDER * width, dtype=F32)).reshape(HY_ORDER, 1, width)
    h = h * jnp.exp(-t[:, :, None, None] * deltas)
    fwd, bwd = h[:, :, 0], h[:, :, 1]
    bwd = bwd.at[0].set(0.0)
    scale = lax.rsqrt(jnp.sum(fwd * fwd, axis=0) + jnp.sum(bwd * bwd, axis=0) + EPS)
    return (fwd * scale).reshape(L, -1), (bwd * scale).reshape(L, -1)


def filter_spectrum(L, filt, width, dft_fwd):
    fwd, bwd = hyena_filters(L, *filt, width)
    spec = left_matmul(dft_fwd, jnp.stack([fwd, bwd]))
    r = lax.broadcasted_iota(jnp.int32, (2 * L, 1), 0)
    return (spec[0] + jnp.where(r <= L, 1.0, -1.0) * spec[1]).reshape(2, L, -1)


def _dft_mul_kernel(m_ref, u_ref, k_ref, o_ref):
    i = pl.program_id(2)
    tm = o_ref.shape[1]
    u = u_ref[...]
    za = jnp.dot(m_ref[0], u, preferred_element_type=F32)
    zb = jnp.dot(m_ref[1], u, preferred_element_type=F32)
    ka, kb = k_ref[0], k_ref[1]
    first = (i * tm + lax.broadcasted_iota(jnp.int32, za.shape, 0)) == 0
    bb = zb * kb
    o_ref[0] = (za * ka - jnp.where(first, 0.0, bb)).astype(o_ref.dtype)
    o_ref[1] = jnp.where(first, bb, za * kb + zb * ka).astype(o_ref.dtype)


def dft_multiply(dft_fwd, u, u_col0, k_spec, k_col0, n_seq, width, *, tm=512, tn=512):
    L = dft_fwd.shape[1]
    tm, tn = _tile(L, tm), _tile(width, tn)
    assert u_col0 % tn == 0 and k_col0 % tn == 0 and u.shape[0] == n_seq * L
    ucb, kcb = u_col0 // tn, k_col0 // tn
    return pl.pallas_call(
        _dft_mul_kernel,
        grid=(n_seq, width // tn, L // tm),
        in_specs=[pl.BlockSpec((2, tm, L), lambda b, j, i: (0, i, 0)),
                  pl.BlockSpec((L, tn), lambda b, j, i: (b, ucb + j)),
                  pl.BlockSpec((2, tm, tn), lambda b, j, i: (0, i, kcb + j))],
        out_specs=pl.BlockSpec((None, 2, tm, tn), lambda b, j, i: (b, 0, i, j)),
        out_shape=jax.ShapeDtypeStruct((n_seq, 2, L, width), BF16),
        compiler_params=_params(3),
    )(dft_fwd, u, k_spec)


def _idft_gate_kernel(m_ref, y_ref, u_ref, skip_ref, gate_ref, o_ref):
    y = (jnp.dot(m_ref[0], y_ref[0], preferred_element_type=F32)
         + jnp.dot(m_ref[1], y_ref[1], preferred_element_type=F32))
    y = y + u_ref[...].astype(F32) * skip_ref[...]
    o_ref[...] = (gate_ref[...].astype(F32) * y).astype(o_ref.dtype)


def idft_gate(dft_inv, y_spec, u, u_col0, skip, skip_row, gate, gate_col0, *, tm=512, tn=512):
    L = dft_inv.shape[1]
    n_seq, _, _, width = y_spec.shape
    tm, tn = _tile(L, tm), _tile(width, tn)
    assert u_col0 % tn == 0 and gate_col0 % tn == 0
    ucb, gcb, tiles = u_col0 // tn, gate_col0 // tn, L // tm
    return pl.pallas_call(
        _idft_gate_kernel,
        grid=(n_seq, width // tn, tiles),
        in_specs=[pl.BlockSpec((2, tm, L), lambda b, j, i: (0, i, 0)),
                  pl.BlockSpec((None, 2, L, tn), lambda b, j, i: (b, 0, 0, j)),
                  pl.BlockSpec((tm, tn), lambda b, j, i: (b * tiles + i, ucb + j)),
                  pl.BlockSpec((None, 1, tn), lambda b, j, i: (skip_row, 0, j)),
                  pl.BlockSpec((tm, tn), lambda b, j, i: (b * tiles + i, gcb + j))],
        out_specs=pl.BlockSpec((tm, tn), lambda b, j, i: (b * tiles + i, j)),
        out_shape=jax.ShapeDtypeStruct((n_seq * L, width), BF16),
        compiler_params=_params(3),
    )(dft_inv, y_spec, u, skip, gate)


def _short_conv_kernel(p_ref, w_ref, b_ref, o_ref):
    x = p_ref[...].astype(F32)
    n = x.shape[0]
    row = lax.broadcasted_iota(jnp.int32, x.shape, 0)
    prev = jnp.where(row == 0, 0.0, pltpu.roll(x, 1, 0))
    nxt = jnp.where(row == n - 1, 0.0, pltpu.roll(x, n - 1, 0))
    o_ref[...] = (prev * w_ref[0:1, :] + x * w_ref[1:2, :] + nxt * w_ref[2:3, :] + b_ref[...]).astype(o_ref.dtype)


def short_conv3(p, row0, n_seq, L, width, w, b, layer, *, tc=256):
    tc = _tile(width, tc)
    assert row0 % L == 0
    rb0 = row0 // L
    return pl.pallas_call(
        _short_conv_kernel,
        grid=(n_seq, width // tc),
        in_specs=[pl.BlockSpec((L, tc), lambda s, j: (rb0 + s, j)),
                  pl.BlockSpec((None, 3, tc), lambda s, j: (layer, 0, j)),
                  pl.BlockSpec((None, 1, tc), lambda s, j: (layer, 0, j))],
        out_specs=pl.BlockSpec((L, tc), lambda s, j: (s, j)),
        out_shape=jax.ShapeDtypeStruct((n_seq * L, width), BF16),
        compiler_params=_params(2),
    )(p, w, b.reshape(b.shape[0], 1, width))


def hyena_mixer(p, row0, n_seq, L, width, conv_w, conv_b, filt, skip, layer):
    vxx = short_conv3(p, row0, n_seq, L, (HY_ORDER + 1) * width, conv_w, conv_b, layer)
    dft_fwd, dft_inv = dft_matrices(L)
    k_spec = filter_spectrum(L, filt, width, dft_fwd)
    skip = skip.reshape(-1, 1, width)
    spec = dft_multiply(dft_fwd, vxx, 0, k_spec, 0, n_seq, width)
    z = idft_gate(dft_inv, spec, vxx, 0, skip, layer * HY_ORDER, vxx, width)
    spec = dft_multiply(dft_fwd, z, 0, k_spec, width, n_seq, width)
    return idft_gate(dft_inv, spec, z, 0, skip, layer * HY_ORDER + 1, vxx, 2 * width)


def _gmlp_kernel(u_ref, v_ref, g_ref, ws_ref, bs_ref, o_ref):
    v = v_ref[...].astype(F32)
    vc = v - jnp.mean(v, axis=-1, keepdims=True)
    vn = vc * lax.rsqrt(jnp.mean(vc * vc, axis=-1, keepdims=True) + EPS) * g_ref[...]
    vn = vn.astype(BF16)
    gd = v.shape[1] // GM_GROUPS
    for g in range(GM_GROUPS):
        cols = slice(g * gd, (g + 1) * gd)
        s = jnp.dot(ws_ref[g].astype(BF16), vn[:, cols], preferred_element_type=F32)
        o_ref[:, cols] = (u_ref[:, cols].astype(F32) * (s + bs_ref[:, cols])).astype(o_ref.dtype)


def chunk_gmlp(p, col0, width, ln_g, ws, bs, layer):
    M = p.shape[0]
    assert col0 % width == 0 and M % GM_CHUNK == 0
    cb = col0 // width
    bs_full = jnp.repeat(bs[layer].T, width // GM_GROUPS, axis=1)
    return pl.pallas_call(
        _gmlp_kernel,
        grid=(M // GM_CHUNK,),
        in_specs=[pl.BlockSpec((GM_CHUNK, width), lambda n: (n, cb)),
                  pl.BlockSpec((GM_CHUNK, width), lambda n: (n, cb + 1)),
                  pl.BlockSpec((None, 1, width), lambda n: (layer, 0, 0)),
                  pl.BlockSpec((None, GM_GROUPS, GM_CHUNK, GM_CHUNK), lambda n: (layer, 0, 0, 0)),
                  pl.BlockSpec((GM_CHUNK, width), lambda n: (0, 0))],
        out_specs=pl.BlockSpec((GM_CHUNK, width), lambda n: (n, 0)),
        out_shape=jax.ShapeDtypeStruct((M, width), BF16),
        compiler_params=_params(1),
    )(p, p, ln_g.reshape(ln_g.shape[0], 1, width), ws, bs_full)


def _attend(lam, q, kv_refs, g_ref, o_ref, out_scale):
    d = q.shape[1] // 2
    maps = []
    for m in range(2):
        cols = slice(m * d, (m + 1) * d)
        ss = [lax.dot_general(q[:, cols], k_ref[:, cols], (((1,), (1,)), ((), ())), preferred_element_type=F32)
              for k_ref, _ in kv_refs]
        mx = functools.reduce(jnp.maximum, [jnp.max(s, axis=-1, keepdims=True) for s in ss])
        es = [jnp.exp2(s - mx) for s in ss]
        total = functools.reduce(jnp.add, [jnp.sum(e, axis=-1, keepdims=True) for e in es])
        maps.append((es, total))
    (e0, l0), (e1, l1) = maps
    r0, r1 = 1.0 / l0, lam / l1
    o = None
    for piece, (_, v_ref) in enumerate(kv_refs):
        a = (e0[piece] * r0 - e1[piece] * r1).astype(BF16)
        pv = jnp.dot(a, v_ref[...], preferred_element_type=F32)
        o = pv if o is None else o + pv
    o = o * lax.rsqrt(jnp.mean(o * o, axis=-1, keepdims=True) + EPS)
    o_ref[...] = (o * (g_ref[...] * out_scale)).astype(o_ref.dtype)


def _attn_kernel(lam_ref, q_ref, kl_ref, vl_ref, kc_ref, vc_ref, g_ref, o_ref, *, n_lat_q, out_scale):
    i = pl.program_id(2)
    lam = lam_ref[0]

    @pl.when(i < n_lat_q)
    def _():
        _attend(lam, q_ref[...], [(kl_ref, vl_ref), (kc_ref, vc_ref)], g_ref, o_ref, out_scale)

    @pl.when(i >= n_lat_q)
    def _():
        _attend(lam, q_ref[...], [(kc_ref, vc_ref)], g_ref, o_ref, out_scale)


def diff_attention(qkv, lam, subln_g, layer, out_scale, stream, *, tq=256):
    M = qkv.shape[0]
    B, L, n_ctx = stream.B, stream.L, stream.n_ctx
    hw = 2 * DA_HEAD_DIM
    W = DA_HEADS * hw
    assert qkv.shape[1] == 3 * W
    tq = _tile(math.gcd(L, n_ctx), tq, 8)
    n_lat_q, n_ctx_q = L // tq, n_ctx // tq
    assert (B * L) % n_ctx == 0
    ctx_block0 = (B * L) // n_ctx

    def q_index(b, h, i):
        return (jnp.where(i < n_lat_q, b * n_lat_q + i, B * n_lat_q + b * n_ctx_q + (i - n_lat_q)), h)

    kernel = functools.partial(_attn_kernel, n_lat_q=n_lat_q, out_scale=out_scale)
    return pl.pallas_call(
        kernel,
        grid=(B, DA_HEADS, n_lat_q + n_ctx_q),
        in_specs=[pl.BlockSpec(memory_space=pltpu.SMEM),
                  pl.BlockSpec((tq, hw), q_index),
                  pl.BlockSpec((L, hw), lambda b, h, i: (b, DA_HEADS + h)),
                  pl.BlockSpec((L, hw), lambda b, h, i: (b, 2 * DA_HEADS + h)),
                  pl.BlockSpec((n_ctx, hw), lambda b, h, i: (ctx_block0 + b, DA_HEADS + h)),
                  pl.BlockSpec((n_ctx, hw), lambda b, h, i: (ctx_block0 + b, 2 * DA_HEADS + h)),
                  pl.BlockSpec((None, 1, hw), lambda b, h, i: (layer, 0, 0))],
        out_specs=pl.BlockSpec((tq, hw), q_index),
        out_shape=jax.ShapeDtypeStruct((M, W), BF16),
        compiler_params=_params(3),
    )(lam, qkv, qkv, qkv, qkv, qkv, subln_g.reshape(subln_g.shape[0], 1, hw))


def _router_kernel(x_ref, ada_ref, g_ref, wr_ref, br_ref, h_ref, ti_ref, tg_ref, *, stream, shift_col, scale_col):
    tm = x_ref.shape[0]
    n_exp = wr_ref.shape[1]
    row = stream.mod_row(pl.program_id(0) * tm)
    h = _modulated_norm(x_ref[...], g_ref[...], ada_ref, row, shift_col, scale_col)
    half = h.shape[1] // 2
    bits = pltpu.bitcast(h.astype(BF16).astype(F32), jnp.uint32)
    h_ref[...] = (bits[:, :half] >> 16) | (bits[:, half:] & jnp.uint32(0xFFFF0000))
    logits = jnp.dot(h, wr_ref[...], precision=lax.Precision.HIGHEST, preferred_element_type=F32) + br_ref[...]

    lane = lax.broadcasted_iota(jnp.int32, logits.shape, 1)
    out_lane = lax.broadcasted_iota(jnp.int32, ti_ref.shape, 1)
    top_i = jnp.zeros(ti_ref.shape, jnp.int32)
    top_v = []
    vals = logits
    for k in range(TOP_K):
        m = jnp.max(vals, axis=-1, keepdims=True)
        sel = jnp.min(jnp.where(vals == m, lane, n_exp), axis=-1, keepdims=True)
        top_i = jnp.where(out_lane == k, sel, top_i)
        top_v.append(m)
        vals = jnp.where(lane == sel, -jnp.inf, vals)
    ti_ref[...] = top_i

    es = [jnp.exp(v - top_v[0]) for v in top_v]
    total = functools.reduce(jnp.add, es)
    gates = jnp.zeros(tg_ref.shape, F32)
    for k in range(TOP_K):
        gates = jnp.where(out_lane == k, es[k] / total, gates)
    tg_ref[...] = gates


def route(x, n_rows, ada, g, wr, br, layer, stream, *, shift_col, scale_col, tm=256):
    D = x.shape[1]
    E = wr.shape[2]
    tm = stream.row_tile(tm)
    assert n_rows % tm == 0
    kernel = functools.partial(_router_kernel, stream=stream, shift_col=shift_col, scale_col=scale_col)
    h, top_i, gates = pl.pallas_call(
        kernel,
        grid=(n_rows // tm,),
        in_specs=[pl.BlockSpec((tm, D), lambda i: (i, 0)),
                  pl.BlockSpec(ada.shape, lambda i: (0, 0)),
                  pl.BlockSpec((1, D), lambda i: (0, 0)),
                  pl.BlockSpec((None, D, E), lambda i: (layer, 0, 0)),
                  pl.BlockSpec((None, 1, E), lambda i: (layer, 0, 0))],
        out_specs=[pl.BlockSpec((tm, D // 2), lambda i: (i, 0)),
                   pl.BlockSpec((tm, LANES), lambda i: (i, 0)),
                   pl.BlockSpec((tm, LANES), lambda i: (i, 0))],
        out_shape=[jax.ShapeDtypeStruct((n_rows, D // 2), jnp.uint32),
                   jax.ShapeDtypeStruct((n_rows, LANES), jnp.int32),
                   jax.ShapeDtypeStruct((n_rows, LANES), F32)],
        compiler_params=_params(1),
    )(x, ada, g.reshape(1, D), wr, br.reshape(br.shape[0], 1, E))
    return h, top_i[:, :TOP_K], gates[:, :TOP_K]


def _moe_kernel(te_ref, nt_ref, tok_ref, h_ref, g_ref, w1_ref, b1_ref, w2_ref, b2_ref, o_ref,
                x_buf0, x_buf1, x_sem, w1_scr, w2_scr, *, n_chunks, n_tiles):
    t = pl.program_id(0)
    tm = o_ref.shape[0]
    ff = w2_ref.shape[0]
    fc = ff // n_chunks
    x_bufs = (x_buf0, x_buf1)

    def start_gather(tile, dst):
        for r in range(tm):
            tok = tok_ref[tile * tm + r]
            pltpu.make_async_copy(h_ref.at[tok], x_bufs[dst].at[r], x_sem.at[dst]).start()

    def wait_gather(dst):
        pltpu.make_async_copy(h_ref.at[pl.ds(0, tm)], x_bufs[dst], x_sem.at[dst]).wait()

    @pl.when(t == 0)
    def _():
        start_gather(0, 0)

    new_expert = jnp.logical_or(t == 0, te_ref[t] != te_ref[jnp.maximum(t - 1, 0)])

    used = t < nt_ref[0]

    @pl.when(jnp.logical_and(used, new_expert))
    def _():
        for c in range(2 * n_chunks):
            cols = slice(c * fc, (c + 1) * fc)
            w1_scr[:, cols] = w1_ref[:, cols].astype(BF16)
        for c in range(n_chunks):
            rows = slice(c * fc, (c + 1) * fc)
            w2_scr[rows, :] = w2_ref[rows, :].astype(BF16)

    def expert_tile(cur):
        x_buf = x_bufs[cur]
        wait_gather(cur)
        words = jnp.concatenate([x_buf[:, s, :] for s in range(x_buf.shape[1])], axis=-1)
        start_gather(jnp.minimum(t + 1, n_tiles - 1), 1 - cur)
        half = words.shape[1]
        x_lo = pltpu.bitcast(words << 16, F32).astype(BF16)
        x_hi = pltpu.bitcast(words & jnp.uint32(0xFFFF0000), F32).astype(BF16)

        def x_dot(w_cols):
            return (jnp.dot(x_lo, w1_scr[:half, w_cols], preferred_element_type=F32)
                    + jnp.dot(x_hi, w1_scr[half:, w_cols], preferred_element_type=F32))

        y = jnp.zeros(o_ref.shape, F32)
        for c in range(n_chunks):
            glu_cols = slice(c * fc, (c + 1) * fc)
            lin_cols = slice(ff + c * fc, ff + (c + 1) * fc)
            glu = x_dot(glu_cols) + b1_ref[:, glu_cols]
            lin = x_dot(lin_cols) + b1_ref[:, lin_cols]
            glu = jnp.minimum(glu, SWIGLU_LIMIT)
            lin = jnp.clip(lin, -SWIGLU_LIMIT, SWIGLU_LIMIT)
            act = glu * jax.nn.sigmoid(SWIGLU_ALPHA * glu) * (lin + 1.0)
            y = y + jnp.dot(act.astype(BF16), w2_scr[glu_cols, :], preferred_element_type=F32)
        o_ref[...] = ((y + b2_ref[...]) * g_ref[...]).astype(o_ref.dtype)

    def unused_tile(cur):
        wait_gather(cur)
        start_gather(jnp.minimum(t + 1, n_tiles - 1), 1 - cur)
        o_ref[...] = jnp.zeros(o_ref.shape, o_ref.dtype)

    for cur in range(2):
        parity = t % 2 == cur
        pl.when(jnp.logical_and(parity, used))(functools.partial(expert_tile, cur))
        pl.when(jnp.logical_and(parity, jnp.logical_not(used)))(functools.partial(unused_tile, cur))
        pl.when(jnp.logical_and(parity, t == n_tiles - 1))(functools.partial(wait_gather, 1 - cur))


def moe_experts(h, row_token, row_gate, tile_expert, n_tiles_used, w1, b1, w2, b2, layer, *, tm):
    N = h.shape[0]
    D = w1.shape[2]
    P = row_token.shape[0]
    assert h.shape[1] * 2 == D and h.shape[1] % LANES == 0
    h = h.reshape(N, h.shape[1] // LANES, LANES)
    E, _, F2 = w1.shape[1:]
    ff = F2 // 2
    n_tiles = P // tm
    kernel = functools.partial(_moe_kernel, n_chunks=3, n_tiles=n_tiles)
    grid_spec = pltpu.PrefetchScalarGridSpec(
        num_scalar_prefetch=3,
        grid=(n_tiles,),
        in_specs=[pl.BlockSpec(memory_space=pl.ANY),
                  pl.BlockSpec((tm, 1), lambda t, te, nt, tok: (t, 0)),
                  pl.BlockSpec((None, None, D, F2), lambda t, te, nt, tok: (layer, te[t], 0, 0)),
                  pl.BlockSpec((None, None, 1, F2), lambda t, te, nt, tok: (layer, te[t], 0, 0)),
                  pl.BlockSpec((None, None, ff, D), lambda t, te, nt, tok: (layer, te[t], 0, 0)),
                  pl.BlockSpec((None, None, 1, D), lambda t, te, nt, tok: (layer, te[t], 0, 0))],
        out_specs=pl.BlockSpec((tm, D), lambda t, te, nt, tok: (t, 0)),
        scratch_shapes=[pltpu.VMEM((tm,) + h.shape[1:], jnp.uint32), pltpu.VMEM((tm,) + h.shape[1:], jnp.uint32),
                        pltpu.SemaphoreType.DMA((2,)),
                        pltpu.VMEM((D, F2), BF16), pltpu.VMEM((ff, D), BF16)],
    )
    return pl.pallas_call(
        kernel,
        grid_spec=grid_spec,
        out_shape=jax.ShapeDtypeStruct((P, D), BF16),
        compiler_params=_params(1),
    )(tile_expert, n_tiles_used, row_token, h, row_gate, w1, b1.reshape(b1.shape[0], E, 1, F2), w2,
      b2.reshape(b2.shape[0], E, 1, D))


def moe_ffn(h, top_i, gates, w1, b1, w2, b2, layer, *, tm=256):
    N = h.shape[0]
    E = w1.shape[1]
    P = N * TOP_K
    n_tiles = P // tm + E
    i32 = jnp.int32

    pair_e = top_i.reshape(P)
    pair_ids = jnp.arange(P, dtype=i32)
    sorted_e, order = lax.sort((pair_e, pair_ids), num_keys=1, is_stable=True)
    _, rank = lax.sort((order, pair_ids), num_keys=1)
    bounds = jnp.searchsorted(sorted_e, jnp.arange(E + 1, dtype=i32), side="left").astype(i32)
    start_sorted, counts = bounds[:E], bounds[1:] - bounds[:E]
    tiles_per = (counts + tm - 1) // tm
    tile_end = jnp.cumsum(tiles_per)
    tile_start = tile_end - tiles_per
    n_used = tile_end[-1:]
    tile_ids = jnp.minimum(jnp.arange(n_tiles, dtype=i32), n_used - 1)
    tile_expert = jnp.sum((tile_ids[:, None] >= tile_end[None, :]).astype(i32), axis=1)

    all_tiles = jnp.arange(n_tiles, dtype=i32)
    tile_off0 = (all_tiles - tile_start[tile_expert]) * tm
    off = tile_off0[:, None] + jnp.arange(tm, dtype=i32)[None, :]
    valid = (off < counts[tile_expert][:, None]) & (all_tiles < n_used)[:, None]
    sorted_pos = jnp.clip(start_sorted[tile_expert][:, None] + off, 0, P - 1)
    row_pair = order[sorted_pos.reshape(-1)]
    valid = valid.reshape(-1)
    row_token = jnp.where(valid, row_pair // TOP_K, 0)
    row_gate = jnp.where(valid, gates.reshape(P)[row_pair], 0.0)
    pair_row = (tile_start[pair_e] * tm + rank - start_sorted[pair_e]).reshape(N, TOP_K)

    ys = moe_experts(h, row_token, row_gate[:, None], tile_expert, n_used, w1, b1, w2, b2, layer, tm=tm)
    return [ys[pair_row[:, k]] for k in range(TOP_K)]


def _combine_kernel(x_ref, gate_ref, *refs, stream):
    y_refs, o_ref = refs[:-1], refs[-1]
    tm = x_ref.shape[0]
    gate = gate_ref[pl.ds(stream.mod_row(pl.program_id(0) * tm), 1), :]
    y = functools.reduce(jnp.add, [y_ref[...].astype(F32) for y_ref in y_refs])
    o_ref[...] = x_ref[...] + gate * y


def combine_residual(x, ys, ada, gate_col, stream, *, tm=512):
    n, D = ys[0].shape
    tm = stream.row_tile(tm)
    assert n % tm == 0
    row_spec = pl.BlockSpec((tm, D), lambda i: (i, 0))
    return pl.pallas_call(
        functools.partial(_combine_kernel, stream=stream),
        grid=(n // tm,),
        in_specs=[row_spec, pl.BlockSpec((ADA_ROWS, D), lambda i: (0, gate_col))] + [row_spec] * len(ys),
        out_specs=row_spec,
        out_shape=jax.ShapeDtypeStruct((n, D), F32),
        compiler_params=_params(1),
    )(x, ada, *ys)


def rms_norm(x, g):
    return x * lax.rsqrt(jnp.mean(x * x, axis=-1, keepdims=True) + EPS) * g


def kernel(x, c, ctx, c_ctx, ada_w, ada_b, norm_g, final_g, ev_w_in, ev_w_out, hy_conv_w, hy_conv_b,
           hy_f_w1, hy_f_b1, hy_f_w2, hy_f_b2, hy_f_w3, hy_f_b3, hy_f_w4, hy_skip, gm_ln_g, gm_ws, gm_bs,
           od_w_qkv, od_w_out, od_lambda, od_subln_g, moe_wr, moe_br, moe_w1, moe_b1, moe_w2, moe_b2):
    B, L, D = x.shape
    n_ctx = ctx.shape[1]
    depth = ada_w.shape[0]
    hy_width = hy_skip.shape[2]
    hy_split = (HY_ORDER + 1) * hy_width
    gm_width = gm_ln_g.shape[1]
    stream = Stream(B, L, n_ctx)
    n_lat, M = stream.n_lat, stream.rows

    xs = jnp.concatenate([x.reshape(n_lat, D), ctx.reshape(B * n_ctx, D)], axis=0)
    cond = jnp.concatenate([jax.nn.silu(c), jax.nn.silu(c_ctx)[None], jnp.zeros((ADA_ROWS - B - 1, D), F32)], axis=0)
    ev_w_in, ev_w_out, od_w_qkv, od_w_out = (w.astype(BF16) for w in (ev_w_in, ev_w_out, od_w_qkv, od_w_out))

    for l in range(depth):
        i = l // 2
        last = l == depth - 1
        ada = matmul(cond.astype(BF16), ada_w, l, tm=ADA_ROWS, tn=1024) + ada_b[l]

        if l % 2 == 0:
            p = norm_matmul(xs, ada, norm_g[l, 0], ev_w_in, i, stream, shift_col=0, scale_col=1, out_dtype=BF16)
            filt = (hy_f_w1[i], hy_f_b1[i], hy_f_w2[i], hy_f_b2[i], hy_f_w3[i], hy_f_b3[i], hy_f_w4[i])
            y_a = jnp.concatenate(
                [hyena_mixer(p, 0, B, L, hy_width, hy_conv_w, hy_conv_b, filt, hy_skip, i),
                 hyena_mixer(p, n_lat, B, n_ctx, hy_width, hy_conv_w, hy_conv_b, filt, hy_skip, i)], axis=0)
            y_b = chunk_gmlp(p, hy_split, gm_width, gm_ln_g, gm_ws, gm_bs, i)
            xs = matmul_residual([y_a, y_b], ev_w_out, i, xs, ada, 2, stream)
        else:
            lam_init = 0.8 - 0.6 * math.exp(-0.3 * l)
            W = od_w_qkv.shape[2] // 3
            qkv = norm_matmul(xs, ada, norm_g[l, 0], od_w_qkv, i, stream, shift_col=0, scale_col=1, out_dtype=BF16,
                              rope_cols=(2 * W, W, math.log2(math.e) / math.sqrt(DA_HEAD_DIM)))
            lp = od_lambda[i]
            lam = (jnp.exp(jnp.sum(lp[0] * lp[1])) - jnp.exp(jnp.sum(lp[2] * lp[3])) + lam_init).reshape(1)
            o = diff_attention(qkv, lam, od_subln_g, i, 1.0 - lam_init, stream)
            xs = matmul_residual([o], od_w_out, i, xs, ada, 2, stream)

        n_tok = n_lat if last else M
        h, top_i, gates = route(xs, n_tok, ada, norm_g[l, 1], moe_wr, moe_br, l, stream, shift_col=3, scale_col=4)
        outs = moe_ffn(h, top_i, gates, moe_w1, moe_b1, moe_w2, moe_b2, l)
        xs = combine_residual(xs, outs, ada, 5, stream)

    return rms_norm(xs[:n_lat].reshape(B, L, D), final_g)
```

```python
import functools
import math

import jax
import jax.numpy as jnp
from jax import lax
from jax.experimental import pallas as pl
from jax.experimental.pallas import tpu as pltpu

F32 = jnp.float32
BF16 = jnp.bfloat16

EPS = 1e-6
GRID_W = 64

HY_ORDER = 2
HY_EMB = 33
HY_BANDS = (HY_EMB - 1) // 2
HY_DECAY_TARGET = 1e-2
HY_FAST_DECAY = 0.3
HY_SLOW_DECAY = 1.5

GM_GROUPS = 8
GM_CHUNK = 128

DA_HEADS = 8
DA_HEAD_DIM = 128
ROPE_AXIS_DIM = DA_HEAD_DIM // 2
ROPE_THETA = 10000.0

TOP_K = 4
SWIGLU_ALPHA = 1.702
SWIGLU_LIMIT = 7.0

LANES = 128
MIB = 1024 * 1024
VMEM_LIMIT_BYTES = 58 * MIB
ADA_ROWS = 8


def _params(n_grid_dims, vmem=VMEM_LIMIT_BYTES):
    return pltpu.CompilerParams(dimension_semantics=("arbitrary",) * n_grid_dims, vmem_limit_bytes=vmem)


def _tile(n, want, align=LANES):
    if n <= want:
        return n
    return next(t for t in range(want - want % align, 0, -align) if n % t == 0)


class Stream:
    def __init__(self, B, L, n_ctx):
        assert B + 1 <= ADA_ROWS
        self.B, self.L, self.n_ctx = B, L, n_ctx
        self.n_lat = B * L
        self.rows = B * L + B * n_ctx

    def row_tile(self, want):
        return _tile(math.gcd(self.L, self.B * self.n_ctx), want)

    def mod_row(self, row0):
        return jnp.where(row0 < self.n_lat, row0 // self.L, self.B)


def _mm_kernel(a_ref, w_ref, o_ref):
    o_ref[...] = jnp.dot(a_ref[...].astype(BF16), w_ref[...].astype(BF16),
                         preferred_element_type=F32).astype(o_ref.dtype)


def matmul(a, w, layer, *, tm=512, tn=512):
    M, K = a.shape
    N = w.shape[2]
    tm, tn = _tile(M, tm, 8), _tile(N, tn)
    return pl.pallas_call(
        _mm_kernel,
        grid=(N // tn, M // tm),
        in_specs=[pl.BlockSpec((tm, K), lambda j, i: (i, 0)),
                  pl.BlockSpec((None, K, tn), lambda j, i: (layer, 0, j))],
        out_specs=pl.BlockSpec((tm, tn), lambda j, i: (i, j)),
        out_shape=jax.ShapeDtypeStruct((M, N), F32),
        compiler_params=_params(2),
    )(a, w)


def _modulated_norm(x, g, ada_ref, row, shift_col, scale_col):
    D = x.shape[1]
    y = x * lax.rsqrt(jnp.mean(x * x, axis=-1, keepdims=True) + EPS) * g
    scale = ada_ref[pl.ds(row, 1), scale_col * D:(scale_col + 1) * D]
    shift = ada_ref[pl.ds(row, 1), shift_col * D:(shift_col + 1) * D]
    return y * (1.0 + scale) + shift


def _norm_mm_kernel(*refs, stream, shift_col, scale_col, rope):
    if rope is None:
        x_ref, ada_ref, g_ref, w_ref, o_ref, h_scr = refs
    else:
        x_ref, ada_ref, g_ref, w_ref, cos_ref, sin_ref, o_ref, h_scr = refs
    i, j = pl.program_id(0), pl.program_id(1)
    tm, tn = o_ref.shape

    @pl.when(j == 0)
    def _():
        h = _modulated_norm(x_ref[...], g_ref[...], ada_ref, stream.mod_row(i * tm), shift_col, scale_col)
        h_scr[...] = h.astype(BF16)

    y = jnp.dot(h_scr[...], w_ref[...], preferred_element_type=F32)
    if rope is None:
        o_ref[...] = y.astype(o_ref.dtype)
        return

    n_rope_tiles, n_q_tiles, q_scale = rope

    @pl.when(j < n_rope_tiles)
    def _():
        scale = jnp.where(j < n_q_tiles, q_scale, 1.0)
        lane = lax.broadcasted_iota(jnp.int32, (tm, DA_HEAD_DIM), 1)
        first_half = (lane % ROPE_AXIS_DIM) < ROPE_AXIS_DIM // 2
        for hh in range(tn // DA_HEAD_DIM):
            cols = slice(hh * DA_HEAD_DIM, (hh + 1) * DA_HEAD_DIM)
            seg = y[:, cols]
            partner = jnp.where(first_half,
                                pltpu.roll(seg, DA_HEAD_DIM - ROPE_AXIS_DIM // 2, 1),
                                pltpu.roll(seg, ROPE_AXIS_DIM // 2, 1))
            o_ref[:, cols] = ((seg * cos_ref[...] + partner * sin_ref[...]) * scale).astype(o_ref.dtype)

    @pl.when(j >= n_rope_tiles)
    def _():
        o_ref[...] = y.astype(o_ref.dtype)


def rope_tables(L, tile_rows):
    pos = jnp.arange(L)
    inv = ROPE_THETA ** (-jnp.arange(0, ROPE_AXIS_DIM, 2, dtype=F32) / ROPE_AXIS_DIM)
    ar = (pos // GRID_W).astype(F32)[:, None] * inv
    ac = (pos % GRID_W).astype(F32)[:, None] * inv
    cos = jnp.concatenate([jnp.cos(ar), jnp.cos(ar), jnp.cos(ac), jnp.cos(ac)], axis=-1)
    sin = jnp.concatenate([-jnp.sin(ar), jnp.sin(ar), -jnp.sin(ac), jnp.sin(ac)], axis=-1)
    cos = jnp.concatenate([cos, jnp.ones((tile_rows, DA_HEAD_DIM), F32)], axis=0)
    sin = jnp.concatenate([sin, jnp.zeros((tile_rows, DA_HEAD_DIM), F32)], axis=0)
    return cos, sin


def norm_matmul(x, ada, g, w, layer, stream, *, shift_col, scale_col, out_dtype, rope_cols=None, tm=1024, tn=512):
    M, D = x.shape
    N = w.shape[2]
    tm, tn = stream.row_tile(tm), _tile(N, tn)
    in_specs = [pl.BlockSpec((tm, D), lambda i, j: (i, 0)),
                pl.BlockSpec(ada.shape, lambda i, j: (0, 0)),
                pl.BlockSpec((1, D), lambda i, j: (0, 0)),
                pl.BlockSpec((None, D, tn), lambda i, j: (layer, 0, j))]
    args = [x, ada, g.reshape(1, D), w]
    rope = None
    if rope_cols is not None:
        n_rope, n_q, q_scale = rope_cols
        assert n_rope % tn == 0 and n_q % tn == 0 and tn % DA_HEAD_DIM == 0
        rope = (n_rope // tn, n_q // tn, q_scale)
        lat_tiles, tiles_per_sample = stream.n_lat // tm, stream.L // tm
        table_spec = pl.BlockSpec(
            (tm, DA_HEAD_DIM), lambda i, j: (jnp.where(i < lat_tiles, i % tiles_per_sample, tiles_per_sample), 0))
        in_specs += [table_spec, table_spec]
        args += list(rope_tables(stream.L, tm))
    kernel = functools.partial(_norm_mm_kernel, stream=stream, shift_col=shift_col, scale_col=scale_col, rope=rope)
    return pl.pallas_call(
        kernel,
        grid=(M // tm, N // tn),
        in_specs=in_specs,
        out_specs=pl.BlockSpec((tm, tn), lambda i, j: (i, j)),
        out_shape=jax.ShapeDtypeStruct((M, N), out_dtype),
        scratch_shapes=[pltpu.VMEM((tm, D), BF16)],
        compiler_params=_params(2),
    )(*args)


def _mm_res_kernel(*refs, n_a, stream):
    a_refs, w_refs = refs[:n_a], refs[n_a:2 * n_a]
    x_ref, gate_ref, o_ref = refs[2 * n_a:]
    tm = x_ref.shape[0]
    y = jnp.dot(a_refs[0][...], w_refs[0][...], preferred_element_type=F32)
    for a_ref, w_ref in zip(a_refs[1:], w_refs[1:]):
        y = y + jnp.dot(a_ref[...], w_ref[...], preferred_element_type=F32)
    gate = gate_ref[pl.ds(stream.mod_row(pl.program_id(0) * tm), 1), :]
    o_ref[...] = x_ref[...] + gate * y


def matmul_residual(a_list, w, layer, x, ada, gate_col, stream, *, tm=1024, tn=512):
    M, D = a_list[0].shape[0], x.shape[1]
    n_a = len(a_list)
    ka = a_list[0].shape[1]
    assert all(a.shape == (M, ka) for a in a_list) and w.shape[1:] == (n_a * ka, D)
    tm, tn = stream.row_tile(tm), _tile(D, tn)
    gate_block0 = gate_col * (D // tn)
    kernel = functools.partial(_mm_res_kernel, n_a=n_a, stream=stream)
    a_specs = [pl.BlockSpec((tm, ka), lambda i, j: (i, 0)) for _ in a_list]
    w_specs = [pl.BlockSpec((None, ka, tn), functools.partial(lambda i, j, r: (layer, r, j), r=r)) for r in range(n_a)]
    return pl.pallas_call(
        kernel,
        grid=(M // tm, D // tn),
        in_specs=a_specs + w_specs + [pl.BlockSpec((tm, tn), lambda i, j: (i, j)),
                                      pl.BlockSpec((ADA_ROWS, tn), lambda i, j: (0, gate_block0 + j))],
        out_specs=pl.BlockSpec((tm, tn), lambda i, j: (i, j)),
        out_shape=jax.ShapeDtypeStruct((M, D), F32),
        compiler_params=_params(2),
    )(*a_list, *([w] * n_a), x, ada)


def _lmm_kernel(m_ref, u_ref, o_ref):
    o_ref[...] = jnp.dot(m_ref[...], u_ref[...].astype(BF16), preferred_element_type=F32)


def left_matmul(mat, u, *, tm=512, tn=512):
    H, R, K = mat.shape
    B, _, C = u.shape
    tm, tn = _tile(R, tm), _tile(C, tn)
    tiles = R // tm
    return pl.pallas_call(
        _lmm_kernel,
        grid=(B, C // tn, H * tiles),
        in_specs=[pl.BlockSpec((None, tm, K), lambda b, j, i: (i // tiles, i % tiles, 0)),
                  pl.BlockSpec((None, K, tn), lambda b, j, i: (b, 0, j))],
        out_specs=pl.BlockSpec((None, tm, tn), lambda b, j, i: (b, i, j)),
        out_shape=jax.ShapeDtypeStruct((B, H * R, C), F32),
        compiler_params=_params(3),
    )(mat, u)


def dft_matrices(L):
    n = 2 * L
    step = math.gcd(L, 64)
    freq = jnp.arange(L, dtype=jnp.int32)[:, None]

    def table(times):
        ang = ((freq * times[None, :]) % n).astype(F32) * (2.0 * math.pi / n)
        return jnp.cos(ang), jnp.sin(ang)

    (ca, sa), (cb, sb) = table(step * jnp.arange(L // step, dtype=jnp.int32)), table(jnp.arange(step, dtype=jnp.int32))
    ca, sa, cb, sb = ca[:, :, None], sa[:, :, None], cb[:, None, :], sb[:, None, :]
    re = (ca * cb - sa * sb).reshape(L, L)
    sin = (sa * cb + ca * sb).reshape(L, L)
    f = lax.broadcasted_iota(jnp.int32, (L, L), 0)
    t = lax.broadcasted_iota(jnp.int32, (L, L), 1)
    im = jnp.where(f == 0, (1 - 2 * (t % 2)).astype(F32), -sin)
    weight = jnp.where(jnp.arange(L) == 0, 1.0 / n, 2.0 / n)[None, :]
    fwd = jnp.stack([re, im]).astype(BF16)
    inv = jnp.stack([re * weight, im.T * weight]).astype(BF16)
    return fwd, inv


def hyena_filters(L, w1, b1, w2, b2, w3, b3, w4, width):
    hp = lax.Precision.HIGHEST
    t = jnp.linspace(0.0, 1.0, L, dtype=F32)[:, None]
    w = 2.0 * math.pi * jnp.arange(L, dtype=F32)[:, None] / L
    f = jnp.linspace(1e-4, HY_BANDS - 1, HY_BANDS, dtype=F32)[None]
    z = jnp.concatenate([t, jnp.cos(f * w), -jnp.sin(f * w)], axis=-1)
    h = jnp.sin(jnp.dot(z, w1, precision=hp) + b1)
    h = jnp.sin(jnp.dot(h, w2, precision=hp) + b2)
    h = jnp.sin(jnp.dot(h, w3, precision=hp) + b3)
    h = jnp.dot(h, w4, precision=hp).reshape(L, HY_ORDER, 2, width)
    max_decay = math.log(HY_DECAY_TARGET) / HY_FAST_DECAY
    min_decay = math.log(HY_DECAY_TARGET) / HY_SLOW_DECAY
    deltas = jnp.abs(jnp.linspace(min_decay, max_decay, HY_ORDER * width, dtype=F32)).reshape(HY_ORDER, 1, width)
    h = h * jnp.exp(-t[:, :, None, None] * deltas)
    fwd, bwd = h[:, :, 0], h[:, :, 1]
    bwd = bwd.at[0].set(0.0)
    scale = lax.rsqrt(jnp.sum(fwd * fwd, axis=0) + jnp.sum(bwd * bwd, axis=0) + EPS)
    return (fwd * scale).reshape(L, -1), (bwd * scale).reshape(L, -1)


def filter_spectrum(L, filt, width, dft_fwd):
    fwd, bwd = hyena_filters(L, *filt, width)
    spec = left_matmul(dft_fwd, jnp.stack([fwd, bwd]))
    r = lax.broadcasted_iota(jnp.int32, (2 * L, 1), 0)
    return (spec[0] + jnp.where(r <= L, 1.0, -1.0) * spec[1]).reshape(2, L, -1)


def _dft_mul_kernel(m_ref, u_ref, k_ref, o_ref):
    i = pl.program_id(2)
    tm = o_ref.shape[1]
    u = u_ref[...]
    za = jnp.dot(m_ref[0], u, preferred_element_type=F32)
    zb = jnp.dot(m_ref[1], u, preferred_element_type=F32)
    ka, kb = k_ref[0], k_ref[1]
    first = (i * tm + lax.broadcasted_iota(jnp.int32, za.shape, 0)) == 0
    bb = zb * kb
    o_ref[0] = (za * ka - jnp.where(first, 0.0, bb)).astype(o_ref.dtype)
    o_ref[1] = jnp.where(first, bb, za * kb + zb * ka).astype(o_ref.dtype)


def dft_multiply(dft_fwd, u, u_col0, k_spec, k_col0, n_seq, width, *, tm=512, tn=512):
    L = dft_fwd.shape[1]
    tm, tn = _tile(L, tm), _tile(width, tn)
    assert u_col0 % tn == 0 and k_col0 % tn == 0 and u.shape[0] == n_seq * L
    ucb, kcb = u_col0 // tn, k_col0 // tn
    return pl.pallas_call(
        _dft_mul_kernel,
        grid=(n_seq, width // tn, L // tm),
        in_specs=[pl.BlockSpec((2, tm, L), lambda b, j, i: (0, i, 0)),
                  pl.BlockSpec((L, tn), lambda b, j, i: (b, ucb + j)),
                  pl.BlockSpec((2, tm, tn), lambda b, j, i: (0, i, kcb + j))],
        out_specs=pl.BlockSpec((None, 2, tm, tn), lambda b, j, i: (b, 0, i, j)),
        out_shape=jax.ShapeDtypeStruct((n_seq, 2, L, width), BF16),
        compiler_params=_params(3),
    )(dft_fwd, u, k_spec)


def _idft_gate_kernel(m_ref, y_ref, u_ref, skip_ref, gate_ref, o_ref):
    y = (jnp.dot(m_ref[0], y_ref[0], preferred_element_type=F32)
         + jnp.dot(m_ref[1], y_ref[1], preferred_element_type=F32))
    y = y + u_ref[...].astype(F32) * skip_ref[...]
    o_ref[...] = (gate_ref[...].astype(F32) * y).astype(o_ref.dtype)


def idft_gate(dft_inv, y_spec, u, u_col0, skip, skip_row, gate, gate_col0, *, tm=512, tn=512):
    L = dft_inv.shape[1]
    n_seq, _, _, width = y_spec.shape
    tm, tn = _tile(L, tm), _tile(width, tn)
    assert u_col0 % tn == 0 and gate_col0 % tn == 0
    ucb, gcb, tiles = u_col0 // tn, gate_col0 // tn, L // tm
    return pl.pallas_call(
        _idft_gate_kernel,
        grid=(n_seq, width // tn, tiles),
        in_specs=[pl.BlockSpec((2, tm, L), lambda b, j, i: (0, i, 0)),
                  pl.BlockSpec((None, 2, L, tn), lambda b, j, i: (b, 0, 0, j)),
                  pl.BlockSpec((tm, tn), lambda b, j, i: (b * tiles + i, ucb + j)),
                  pl.BlockSpec((None, 1, tn), lambda b, j, i: (skip_row, 0, j)),
                  pl.BlockSpec((tm, tn), lambda b, j, i: (b * tiles + i, gcb + j))],
        out_specs=pl.BlockSpec((tm, tn), lambda b, j, i: (b * tiles + i, j)),
        out_shape=jax.ShapeDtypeStruct((n_seq * L, width), BF16),
        compiler_params=_params(3),
    )(dft_inv, y_spec, u, skip, gate)


def _short_conv_kernel(p_ref, w_ref, b_ref, o_ref):
    x = p_ref[...].astype(F32)
    n = x.shape[0]
    row = lax.broadcasted_iota(jnp.int32, x.shape, 0)
    prev = jnp.where(row == 0, 0.0, pltpu.roll(x, 1, 0))
    nxt = jnp.where(row == n - 1, 0.0, pltpu.roll(x, n - 1, 0))
    o_ref[...] = (prev * w_ref[0:1, :] + x * w_ref[1:2, :] + nxt * w_ref[2:3, :] + b_ref[...]).astype(o_ref.dtype)


def short_conv3(p, row0, n_seq, L, width, w, b, layer, *, tc=256):
    tc = _tile(width, tc)
    assert row0 % L == 0
    rb0 = row0 // L
    return pl.pallas_call(
        _short_conv_kernel,
        grid=(n_seq, width // tc),
        in_specs=[pl.BlockSpec((L, tc), lambda s, j: (rb0 + s, j)),
                  pl.BlockSpec((None, 3, tc), lambda s, j: (layer, 0, j)),
                  pl.BlockSpec((None, 1, tc), lambda s, j: (layer, 0, j))],
        out_specs=pl.BlockSpec((L, tc), lambda s, j: (s, j)),
        out_shape=jax.ShapeDtypeStruct((n_seq * L, width), BF16),
        compiler_params=_params(2),
    )(p, w, b.reshape(b.shape[0], 1, width))


def hyena_mixer(p, row0, n_seq, L, width, conv_w, conv_b, filt, skip, layer):
    vxx = short_conv3(p, row0, n_seq, L, (HY_ORDER + 1) * width, conv_w, conv_b, layer)
    dft_fwd, dft_inv = dft_matrices(L)
    k_spec = filter_spectrum(L, filt, width, dft_fwd)
    skip = skip.reshape(-1, 1, width)
    spec = dft_multiply(dft_fwd, vxx, 0, k_spec, 0, n_seq, width)
    z = idft_gate(dft_inv, spec, vxx, 0, skip, layer * HY_ORDER, vxx, width)
    spec = dft_multiply(dft_fwd, z, 0, k_spec, width, n_seq, width)
    return idft_gate(dft_inv, spec, z, 0, skip, layer * HY_ORDER + 1, vxx, 2 * width)


def _gmlp_kernel(u_ref, v_ref, g_ref, ws_ref, bs_ref, o_ref):
    v = v_ref[...].astype(F32)
    vc = v - jnp.mean(v, axis=-1, keepdims=True)
    vn = vc * lax.rsqrt(jnp.mean(vc * vc, axis=-1, keepdims=True) + EPS) * g_ref[...]
    vn = vn.astype(BF16)
    gd = v.shape[1] // GM_GROUPS
    for g in range(GM_GROUPS):
        cols = slice(g * gd, (g + 1) * gd)
        s = jnp.dot(ws_ref[g].astype(BF16), vn[:, cols], preferred_element_type=F32)
        o_ref[:, cols] = (u_ref[:, cols].astype(F32) * (s + bs_ref[:, cols])).astype(o_ref.dtype)


def chunk_gmlp(p, col0, width, ln_g, ws, bs, layer):
    M = p.shape[0]
    assert col0 % width == 0 and M % GM_CHUNK == 0
    cb = col0 // width
    bs_full = jnp.repeat(bs[layer].T, width // GM_GROUPS, axis=1)
    return pl.pallas_call(
        _gmlp_kernel,
        grid=(M // GM_CHUNK,),
        in_specs=[pl.BlockSpec((GM_CHUNK, width), lambda n: (n, cb)),
                  pl.BlockSpec((GM_CHUNK, width), lambda n: (n, cb + 1)),
                  pl.BlockSpec((None, 1, width), lambda n: (layer, 0, 0)),
                  pl.BlockSpec((None, GM_GROUPS, GM_CHUNK, GM_CHUNK), lambda n: (layer, 0, 0, 0)),
                  pl.BlockSpec((GM_CHUNK, width), lambda n: (0, 0))],
        out_specs=pl.BlockSpec((GM_CHUNK, width), lambda n: (n, 0)),
        out_shape=jax.ShapeDtypeStruct((M, width), BF16),
        compiler_params=_params(1),
    )(p, p, ln_g.reshape(ln_g.shape[0], 1, width), ws, bs_full)


def _attend(lam, q, kv_refs, g_ref, o_ref, out_scale):
    d = q.shape[1] // 2
    maps = []
    for m in range(2):
        cols = slice(m * d, (m + 1) * d)
        ss = [lax.dot_general(q[:, cols], k_ref[:, cols], (((1,), (1,)), ((), ())), preferred_element_type=F32)
              for k_ref, _ in kv_refs]
        mx = functools.reduce(jnp.maximum, [jnp.max(s, axis=-1, keepdims=True) for s in ss])
        es = [jnp.exp2(s - mx) for s in ss]
        total = functools.reduce(jnp.add, [jnp.sum(e, axis=-1, keepdims=True) for e in es])
        maps.append((es, total))
    (e0, l0), (e1, l1) = maps
    r0, r1 = 1.0 / l0, lam / l1
    o = None
    for piece, (_, v_ref) in enumerate(kv_refs):
        a = (e0[piece] * r0 - e1[piece] * r1).astype(BF16)
        pv = jnp.dot(a, v_ref[...], preferred_element_type=F32)
        o = pv if o is None else o + pv
    o = o * lax.rsqrt(jnp.mean(o * o, axis=-1, keepdims=True) + EPS)
    o_ref[...] = (o * (g_ref[...] * out_scale)).astype(o_ref.dtype)


def _attn_kernel(lam_ref, q_ref, kl_ref, vl_ref, kc_ref, vc_ref, g_ref, o_ref, *, n_lat_q, out_scale):
    i = pl.program_id(2)
    lam = lam_ref[0]

    @pl.when(i < n_lat_q)
    def _():
        _attend(lam, q_ref[...], [(kl_ref, vl_ref), (kc_ref, vc_ref)], g_ref, o_ref, out_scale)

    @pl.when(i >= n_lat_q)
    def _():
        _attend(lam, q_ref[...], [(kc_ref, vc_ref)], g_ref, o_ref, out_scale)


def diff_attention(qkv, lam, subln_g, layer, out_scale, stream, *, ctx_queries=True, tq=256):
    M = qkv.shape[0] if ctx_queries else stream.n_lat
    B, L, n_ctx = stream.B, stream.L, stream.n_ctx
    hw = 2 * DA_HEAD_DIM
    W = DA_HEADS * hw
    assert qkv.shape[1] == 3 * W
    tq = _tile(math.gcd(L, n_ctx), tq, 8)
    n_lat_q, n_ctx_q = L // tq, (n_ctx // tq if ctx_queries else 0)
    assert (B * L) % n_ctx == 0
    ctx_block0 = (B * L) // n_ctx

    def q_index(b, h, i):
        return (jnp.where(i < n_lat_q, b * n_lat_q + i, B * n_lat_q + b * n_ctx_q + (i - n_lat_q)), h)

    kernel = functools.partial(_attn_kernel, n_lat_q=n_lat_q, out_scale=out_scale)
    return pl.pallas_call(
        kernel,
        grid=(B, DA_HEADS, n_lat_q + n_ctx_q),
        in_specs=[pl.BlockSpec(memory_space=pltpu.SMEM),
                  pl.BlockSpec((tq, hw), q_index),
                  pl.BlockSpec((L, hw), lambda b, h, i: (b, DA_HEADS + h)),
                  pl.BlockSpec((L, hw), lambda b, h, i: (b, 2 * DA_HEADS + h)),
                  pl.BlockSpec((n_ctx, hw), lambda b, h, i: (ctx_block0 + b, DA_HEADS + h)),
                  pl.BlockSpec((n_ctx, hw), lambda b, h, i: (ctx_block0 + b, 2 * DA_HEADS + h)),
                  pl.BlockSpec((None, 1, hw), lambda b, h, i: (layer, 0, 0))],
        out_specs=pl.BlockSpec((tq, hw), q_index),
        out_shape=jax.ShapeDtypeStruct((M, W), BF16),
        compiler_params=_params(3),
    )(lam, qkv, qkv, qkv, qkv, qkv, subln_g.reshape(subln_g.shape[0], 1, hw))


def _router_kernel(x_ref, ada_ref, g_ref, wr_ref, br_ref, h_ref, ti_ref, tg_ref, *, stream, shift_col, scale_col):
    tm = x_ref.shape[0]
    n_exp = wr_ref.shape[1]
    row = stream.mod_row(pl.program_id(0) * tm)
    h = _modulated_norm(x_ref[...], g_ref[...], ada_ref, row, shift_col, scale_col)
    half = h.shape[1] // 2
    bits = pltpu.bitcast(h.astype(BF16).astype(F32), jnp.uint32)
    h_ref[...] = (bits[:, :half] >> 16) | (bits[:, half:] & jnp.uint32(0xFFFF0000))
    logits = jnp.dot(h, wr_ref[...], precision=lax.Precision.HIGHEST, preferred_element_type=F32) + br_ref[...]

    lane = lax.broadcasted_iota(jnp.int32, logits.shape, 1)
    out_lane = lax.broadcasted_iota(jnp.int32, ti_ref.shape, 1)
    top_i = jnp.zeros(ti_ref.shape, jnp.int32)
    top_v = []
    vals = logits
    for k in range(TOP_K):
        m = jnp.max(vals, axis=-1, keepdims=True)
        sel = jnp.min(jnp.where(vals == m, lane, n_exp), axis=-1, keepdims=True)
        top_i = jnp.where(out_lane == k, sel, top_i)
        top_v.append(m)
        vals = jnp.where(lane == sel, -jnp.inf, vals)
    ti_ref[...] = top_i

    es = [jnp.exp(v - top_v[0]) for v in top_v]
    total = functools.reduce(jnp.add, es)
    gates = jnp.zeros(tg_ref.shape, F32)
    for k in range(TOP_K):
        gates = jnp.where(out_lane == k, es[k] / total, gates)
    tg_ref[...] = gates


def route(x, n_rows, ada, g, wr, br, layer, stream, *, shift_col, scale_col, tm=256):
    D = x.shape[1]
    E = wr.shape[2]
    tm = stream.row_tile(tm)
    assert n_rows % tm == 0
    kernel = functools.partial(_router_kernel, stream=stream, shift_col=shift_col, scale_col=scale_col)
    h, top_i, gates = pl.pallas_call(
        kernel,
        grid=(n_rows // tm,),
        in_specs=[pl.BlockSpec((tm, D), lambda i: (i, 0)),
                  pl.BlockSpec(ada.shape, lambda i: (0, 0)),
                  pl.BlockSpec((1, D), lambda i: (0, 0)),
                  pl.BlockSpec((None, D, E), lambda i: (layer, 0, 0)),
                  pl.BlockSpec((None, 1, E), lambda i: (layer, 0, 0))],
        out_specs=[pl.BlockSpec((tm, D // 2), lambda i: (i, 0)),
                   pl.BlockSpec((tm, LANES), lambda i: (i, 0)),
                   pl.BlockSpec((tm, LANES), lambda i: (i, 0))],
        out_shape=[jax.ShapeDtypeStruct((n_rows, D // 2), jnp.uint32),
                   jax.ShapeDtypeStruct((n_rows, LANES), jnp.int32),
                   jax.ShapeDtypeStruct((n_rows, LANES), F32)],
        compiler_params=_params(1),
    )(x, ada, g.reshape(1, D), wr, br.reshape(br.shape[0], 1, E))
    return h, top_i[:, :TOP_K], gates[:, :TOP_K]


def _moe_kernel(te_ref, nt_ref, tok_ref, h_ref, g_ref, w1_ref, b1_ref, w2_ref, b2_ref, o_ref,
                x_buf, x_sem, w1_scr, w2_scr, *, n_chunks, n_tiles):
    t = pl.program_id(0)
    tm = o_ref.shape[0]
    ff = w2_ref.shape[0]
    fc = ff // n_chunks
    slot = t % 2

    def start_gather(tile, dst_slot):
        for r in range(tm):
            tok = tok_ref[tile * tm + r]
            pltpu.make_async_copy(h_ref.at[tok], x_buf.at[dst_slot, r], x_sem.at[dst_slot]).start()

    def wait_gather(dst_slot):
        pltpu.make_async_copy(h_ref.at[pl.ds(0, tm)], x_buf.at[dst_slot], x_sem.at[dst_slot]).wait()

    @pl.when(t == 0)
    def _():
        start_gather(0, 0)

    new_expert = jnp.logical_or(t == 0, te_ref[t] != te_ref[jnp.maximum(t - 1, 0)])

    used = t < nt_ref[0]

    @pl.when(jnp.logical_and(used, new_expert))
    def _():
        for c in range(2 * n_chunks):
            cols = slice(c * fc, (c + 1) * fc)
            w1_scr[:, cols] = w1_ref[:, cols].astype(BF16)
        for c in range(n_chunks):
            rows = slice(c * fc, (c + 1) * fc)
            w2_scr[rows, :] = w2_ref[rows, :].astype(BF16)

    wait_gather(slot)
    start_gather(jnp.minimum(t + 1, n_tiles - 1), 1 - slot)

    @pl.when(used)
    def _():
        words = jnp.concatenate([x_buf[slot, :, s, :] for s in range(x_buf.shape[2])], axis=-1)
        half = words.shape[1]
        x_lo = pltpu.bitcast(words << 16, F32).astype(BF16)
        x_hi = pltpu.bitcast(words & jnp.uint32(0xFFFF0000), F32).astype(BF16)

        def x_dot(w_cols):
            return (jnp.dot(x_lo, w1_scr[:half, w_cols], preferred_element_type=F32)
                    + jnp.dot(x_hi, w1_scr[half:, w_cols], preferred_element_type=F32))

        y = jnp.zeros(o_ref.shape, F32)
        for c in range(n_chunks):
            glu_cols = slice(c * fc, (c + 1) * fc)
            lin_cols = slice(ff + c * fc, ff + (c + 1) * fc)
            glu = x_dot(glu_cols) + b1_ref[:, glu_cols]
            lin = x_dot(lin_cols) + b1_ref[:, lin_cols]
            glu = jnp.minimum(glu, SWIGLU_LIMIT)
            lin = jnp.clip(lin, -SWIGLU_LIMIT, SWIGLU_LIMIT)
            act = glu * jax.nn.sigmoid(SWIGLU_ALPHA * glu) * (lin + 1.0)
            y = y + jnp.dot(act.astype(BF16), w2_scr[glu_cols, :], preferred_element_type=F32)
        o_ref[...] = ((y + b2_ref[...]) * g_ref[...]).astype(o_ref.dtype)

    @pl.when(jnp.logical_not(used))
    def _():
        o_ref[...] = jnp.zeros(o_ref.shape, o_ref.dtype)

    @pl.when(t == n_tiles - 1)
    def _():
        wait_gather(1 - slot)


def moe_experts(h, row_token, row_gate, tile_expert, n_tiles_used, w1, b1, w2, b2, layer, *, tm):
    N = h.shape[0]
    D = w1.shape[2]
    P = row_token.shape[0]
    assert h.shape[1] * 2 == D and h.shape[1] % LANES == 0
    h = h.reshape(N, h.shape[1] // LANES, LANES)
    E, _, F2 = w1.shape[1:]
    ff = F2 // 2
    n_tiles = P // tm
    kernel = functools.partial(_moe_kernel, n_chunks=3, n_tiles=n_tiles)
    grid_spec = pltpu.PrefetchScalarGridSpec(
        num_scalar_prefetch=3,
        grid=(n_tiles,),
        in_specs=[pl.BlockSpec(memory_space=pl.ANY),
                  pl.BlockSpec((tm, 1), lambda t, te, nt, tok: (t, 0)),
                  pl.BlockSpec((None, None, D, F2), lambda t, te, nt, tok: (layer, te[t], 0, 0)),
                  pl.BlockSpec((None, None, 1, F2), lambda t, te, nt, tok: (layer, te[t], 0, 0)),
                  pl.BlockSpec((None, None, ff, D), lambda t, te, nt, tok: (layer, te[t], 0, 0)),
                  pl.BlockSpec((None, None, 1, D), lambda t, te, nt, tok: (layer, te[t], 0, 0))],
        out_specs=pl.BlockSpec((tm, D), lambda t, te, nt, tok: (t, 0)),
        scratch_shapes=[pltpu.VMEM((2, tm) + h.shape[1:], jnp.uint32), pltpu.SemaphoreType.DMA((2,)),
                        pltpu.VMEM((D, F2), BF16), pltpu.VMEM((ff, D), BF16)],
    )
    return pl.pallas_call(
        kernel,
        grid_spec=grid_spec,
        out_shape=jax.ShapeDtypeStruct((P, D), BF16),
        compiler_params=_params(1),
    )(tile_expert, n_tiles_used, row_token, h, row_gate, w1, b1.reshape(b1.shape[0], E, 1, F2), w2,
      b2.reshape(b2.shape[0], E, 1, D))


def moe_ffn(h, top_i, gates, w1, b1, w2, b2, layer, *, tm=256):
    N = h.shape[0]
    E = w1.shape[1]
    P = N * TOP_K
    n_tiles = P // tm + E
    i32 = jnp.int32

    pair_e = top_i.reshape(P)
    pair_ids = jnp.arange(P, dtype=i32)
    sorted_e, order = lax.sort((pair_e, pair_ids), num_keys=1, is_stable=True)
    _, rank = lax.sort((order, pair_ids), num_keys=1)
    bounds = jnp.searchsorted(sorted_e, jnp.arange(E + 1, dtype=i32), side="left").astype(i32)
    start_sorted, counts = bounds[:E], bounds[1:] - bounds[:E]
    tiles_per = (counts + tm - 1) // tm
    tile_end = jnp.cumsum(tiles_per)
    tile_start = tile_end - tiles_per
    n_used = tile_end[-1:]
    tile_ids = jnp.minimum(jnp.arange(n_tiles, dtype=i32), n_used - 1)
    tile_expert = jnp.sum((tile_ids[:, None] >= tile_end[None, :]).astype(i32), axis=1)

    all_tiles = jnp.arange(n_tiles, dtype=i32)
    tile_off0 = (all_tiles - tile_start[tile_expert]) * tm
    off = tile_off0[:, None] + jnp.arange(tm, dtype=i32)[None, :]
    valid = (off < counts[tile_expert][:, None]) & (all_tiles < n_used)[:, None]
    sorted_pos = jnp.clip(start_sorted[tile_expert][:, None] + off, 0, P - 1)
    row_pair = order[sorted_pos.reshape(-1)]
    valid = valid.reshape(-1)
    row_token = jnp.where(valid, row_pair // TOP_K, 0)
    row_gate = jnp.where(valid, gates.reshape(P)[row_pair], 0.0)
    pair_row = (tile_start[pair_e] * tm + rank - start_sorted[pair_e]).reshape(N, TOP_K)

    ys = moe_experts(h, row_token, row_gate[:, None], tile_expert, n_used, w1, b1, w2, b2, layer, tm=tm)
    return [ys[pair_row[:, k]] for k in range(TOP_K)]


def _combine_kernel(x_ref, gate_ref, *refs, stream):
    y_refs, o_ref = refs[:-1], refs[-1]
    tm = x_ref.shape[0]
    gate = gate_ref[pl.ds(stream.mod_row(pl.program_id(0) * tm), 1), :]
    y = functools.reduce(jnp.add, [y_ref[...].astype(F32) for y_ref in y_refs])
    o_ref[...] = x_ref[...] + gate * y


def combine_residual(x, ys, ada, gate_col, stream, *, tm=512):
    n, D = ys[0].shape
    tm = stream.row_tile(tm)
    assert n % tm == 0
    row_spec = pl.BlockSpec((tm, D), lambda i: (i, 0))
    return pl.pallas_call(
        functools.partial(_combine_kernel, stream=stream),
        grid=(n // tm,),
        in_specs=[row_spec, pl.BlockSpec((ADA_ROWS, D), lambda i: (0, gate_col))] + [row_spec] * len(ys),
        out_specs=row_spec,
        out_shape=jax.ShapeDtypeStruct((n, D), F32),
        compiler_params=_params(1),
    )(x, ada, *ys)


def rms_norm(x, g):
    return x * lax.rsqrt(jnp.mean(x * x, axis=-1, keepdims=True) + EPS) * g


def kernel(x, c, ctx, c_ctx, ada_w, ada_b, norm_g, final_g, ev_w_in, ev_w_out, hy_conv_w, hy_conv_b,
           hy_f_w1, hy_f_b1, hy_f_w2, hy_f_b2, hy_f_w3, hy_f_b3, hy_f_w4, hy_skip, gm_ln_g, gm_ws, gm_bs,
           od_w_qkv, od_w_out, od_lambda, od_subln_g, moe_wr, moe_br, moe_w1, moe_b1, moe_w2, moe_b2):
    B, L, D = x.shape
    n_ctx = ctx.shape[1]
    depth = ada_w.shape[0]
    hy_width = hy_skip.shape[2]
    hy_split = (HY_ORDER + 1) * hy_width
    gm_width = gm_ln_g.shape[1]
    stream = Stream(B, L, n_ctx)
    n_lat, M = stream.n_lat, stream.rows

    xs = jnp.concatenate([x.reshape(n_lat, D), ctx.reshape(B * n_ctx, D)], axis=0)
    cond = jnp.concatenate([jax.nn.silu(c), jax.nn.silu(c_ctx)[None], jnp.zeros((ADA_ROWS - B - 1, D), F32)], axis=0)
    ev_w_in, ev_w_out, od_w_qkv, od_w_out = (w.astype(BF16) for w in (ev_w_in, ev_w_out, od_w_qkv, od_w_out))

    for l in range(depth):
        i = l // 2
        last = l == depth - 1
        ada = matmul(cond.astype(BF16), ada_w, l, tm=ADA_ROWS, tn=1024) + ada_b[l]

        if l % 2 == 0:
            p = norm_matmul(xs, ada, norm_g[l, 0], ev_w_in, i, stream, shift_col=0, scale_col=1, out_dtype=BF16)
            filt = (hy_f_w1[i], hy_f_b1[i], hy_f_w2[i], hy_f_b2[i], hy_f_w3[i], hy_f_b3[i], hy_f_w4[i])
            y_a = jnp.concatenate(
                [hyena_mixer(p, 0, B, L, hy_width, hy_conv_w, hy_conv_b, filt, hy_skip, i),
                 hyena_mixer(p, n_lat, B, n_ctx, hy_width, hy_conv_w, hy_conv_b, filt, hy_skip, i)], axis=0)
            y_b = chunk_gmlp(p, hy_split, gm_width, gm_ln_g, gm_ws, gm_bs, i)
            xs = matmul_residual([y_a, y_b], ev_w_out, i, xs, ada, 2, stream)
        else:
            lam_init = 0.8 - 0.6 * math.exp(-0.3 * l)
            W = od_w_qkv.shape[2] // 3
            qkv = norm_matmul(xs, ada, norm_g[l, 0], od_w_qkv, i, stream, shift_col=0, scale_col=1, out_dtype=BF16,
                              rope_cols=(2 * W, W, math.log2(math.e) / math.sqrt(DA_HEAD_DIM)))
            lp = od_lambda[i]
            lam = (jnp.exp(jnp.sum(lp[0] * lp[1])) - jnp.exp(jnp.sum(lp[2] * lp[3])) + lam_init).reshape(1)
            o = diff_attention(qkv, lam, od_subln_g, i, 1.0 - lam_init, stream, ctx_queries=not last)
            xs = matmul_residual([o], od_w_out, i, xs, ada, 2, stream)

        n_tok = n_lat if last else M
        h, top_i, gates = route(xs, n_tok, ada, norm_g[l, 1], moe_wr, moe_br, l, stream, shift_col=3, scale_col=4)
        outs = moe_ffn(h, top_i, gates, moe_w1, moe_b1, moe_w2, moe_b2, l)
        xs = combine_residual(xs, outs, ada, 5, stream)

    return rms_norm(xs[:n_lat].reshape(B, L, D), final_g)
```

```python
import functools
import math

import jax
import jax.numpy as jnp
from jax import lax
from jax.experimental import pallas as pl
from jax.experimental.pallas import tpu as pltpu

F32 = jnp.float32
BF16 = jnp.bfloat16

EPS = 1e-6
GRID_W = 64

HY_ORDER = 2
HY_EMB = 33
HY_BANDS = (HY_EMB - 1) // 2
HY_DECAY_TARGET = 1e-2
HY_FAST_DECAY = 0.3
HY_SLOW_DECAY = 1.5

GM_GROUPS = 8
GM_CHUNK = 128

DA_HEADS = 8
DA_HEAD_DIM = 128
ROPE_AXIS_DIM = DA_HEAD_DIM // 2
ROPE_THETA = 10000.0

TOP_K = 4
SWIGLU_ALPHA = 1.702
SWIGLU_LIMIT = 7.0

LANES = 128
MIB = 1024 * 1024
VMEM_LIMIT_BYTES = 58 * MIB
ADA_ROWS = 8


def _params(n_grid_dims, vmem=VMEM_LIMIT_BYTES):
    return pltpu.CompilerParams(dimension_semantics=("arbitrary",) * n_grid_dims, vmem_limit_bytes=vmem)


def _tile(n, want, align=LANES):
    if n <= want:
        return n
    return next(t for t in range(want - want % align, 0, -align) if n % t == 0)


class Stream:
    def __init__(self, B, L, n_ctx):
        assert B + 1 <= ADA_ROWS
        self.B, self.L, self.n_ctx = B, L, n_ctx
        self.n_lat = B * L
        self.rows = B * L + B * n_ctx

    def row_tile(self, want):
        return _tile(math.gcd(self.L, self.B * self.n_ctx), want)

    def mod_row(self, row0):
        return jnp.where(row0 < self.n_lat, row0 // self.L, self.B)


def _mm_kernel(a_ref, w_ref, o_ref):
    o_ref[...] = jnp.dot(a_ref[...].astype(BF16), w_ref[...].astype(BF16),
                         preferred_element_type=F32).astype(o_ref.dtype)


def matmul(a, w, layer, *, tm=512, tn=512):
    M, K = a.shape
    N = w.shape[2]
    tm, tn = _tile(M, tm, 8), _tile(N, tn)
    return pl.pallas_call(
        _mm_kernel,
        grid=(N // tn, M // tm),
        in_specs=[pl.BlockSpec((tm, K), lambda j, i: (i, 0)),
                  pl.BlockSpec((None, K, tn), lambda j, i: (layer, 0, j))],
        out_specs=pl.BlockSpec((tm, tn), lambda j, i: (i, j)),
        out_shape=jax.ShapeDtypeStruct((M, N), F32),
        compiler_params=_params(2),
    )(a, w)


def _modulated_norm(x, g, ada_ref, row, shift_col, scale_col):
    D = x.shape[1]
    y = x * lax.rsqrt(jnp.mean(x * x, axis=-1, keepdims=True) + EPS) * g
    scale = ada_ref[pl.ds(row, 1), scale_col * D:(scale_col + 1) * D]
    shift = ada_ref[pl.ds(row, 1), shift_col * D:(shift_col + 1) * D]
    return y * (1.0 + scale) + shift


def _norm_mm_kernel(*refs, stream, shift_col, scale_col, rope):
    if rope is None:
        x_ref, ada_ref, g_ref, w_ref, o_ref, h_scr = refs
    else:
        x_ref, ada_ref, g_ref, w_ref, cos_ref, sin_ref, o_ref, h_scr = refs
    i, j = pl.program_id(0), pl.program_id(1)
    tm, tn = o_ref.shape

    @pl.when(j == 0)
    def _():
        h = _modulated_norm(x_ref[...], g_ref[...], ada_ref, stream.mod_row(i * tm), shift_col, scale_col)
        h_scr[...] = h.astype(BF16)

    y = jnp.dot(h_scr[...], w_ref[...], preferred_element_type=F32)
    if rope is None:
        o_ref[...] = y.astype(o_ref.dtype)
        return

    n_rope_tiles, n_q_tiles, q_scale = rope

    @pl.when(j < n_rope_tiles)
    def _():
        scale = jnp.where(j < n_q_tiles, q_scale, 1.0)
        lane = lax.broadcasted_iota(jnp.int32, (tm, DA_HEAD_DIM), 1)
        first_half = (lane % ROPE_AXIS_DIM) < ROPE_AXIS_DIM // 2
        for hh in range(tn // DA_HEAD_DIM):
            cols = slice(hh * DA_HEAD_DIM, (hh + 1) * DA_HEAD_DIM)
            seg = y[:, cols]
            partner = jnp.where(first_half,
                                pltpu.roll(seg, DA_HEAD_DIM - ROPE_AXIS_DIM // 2, 1),
                                pltpu.roll(seg, ROPE_AXIS_DIM // 2, 1))
            o_ref[:, cols] = ((seg * cos_ref[...] + partner * sin_ref[...]) * scale).astype(o_ref.dtype)

    @pl.when(j >= n_rope_tiles)
    def _():
        o_ref[...] = y.astype(o_ref.dtype)


def rope_tables(L, tile_rows):
    pos = jnp.arange(L)
    inv = ROPE_THETA ** (-jnp.arange(0, ROPE_AXIS_DIM, 2, dtype=F32) / ROPE_AXIS_DIM)
    ar = (pos // GRID_W).astype(F32)[:, None] * inv
    ac = (pos % GRID_W).astype(F32)[:, None] * inv
    cos = jnp.concatenate([jnp.cos(ar), jnp.cos(ar), jnp.cos(ac), jnp.cos(ac)], axis=-1)
    sin = jnp.concatenate([-jnp.sin(ar), jnp.sin(ar), -jnp.sin(ac), jnp.sin(ac)], axis=-1)
    cos = jnp.concatenate([cos, jnp.ones((tile_rows, DA_HEAD_DIM), F32)], axis=0)
    sin = jnp.concatenate([sin, jnp.zeros((tile_rows, DA_HEAD_DIM), F32)], axis=0)
    return cos, sin


def norm_matmul(x, ada, g, w, layer, stream, *, shift_col, scale_col, out_dtype, rope_cols=None, tm=1024, tn=512):
    M, D = x.shape
    N = w.shape[2]
    tm, tn = stream.row_tile(tm), _tile(N, tn)
    in_specs = [pl.BlockSpec((tm, D), lambda i, j: (i, 0)),
                pl.BlockSpec(ada.shape, lambda i, j: (0, 0)),
                pl.BlockSpec((1, D), lambda i, j: (0, 0)),
                pl.BlockSpec((None, D, tn), lambda i, j: (layer, 0, j))]
    args = [x, ada, g.reshape(1, D), w]
    rope = None
    if rope_cols is not None:
        n_rope, n_q, q_scale = rope_cols
        assert n_rope % tn == 0 and n_q % tn == 0 and tn % DA_HEAD_DIM == 0
        rope = (n_rope // tn, n_q // tn, q_scale)
        lat_tiles, tiles_per_sample = stream.n_lat // tm, stream.L // tm
        table_spec = pl.BlockSpec(
            (tm, DA_HEAD_DIM), lambda i, j: (jnp.where(i < lat_tiles, i % tiles_per_sample, tiles_per_sample), 0))
        in_specs += [table_spec, table_spec]
        args += list(rope_tables(stream.L, tm))
    kernel = functools.partial(_norm_mm_kernel, stream=stream, shift_col=shift_col, scale_col=scale_col, rope=rope)
    return pl.pallas_call(
        kernel,
        grid=(M // tm, N // tn),
        in_specs=in_specs,
        out_specs=pl.BlockSpec((tm, tn), lambda i, j: (i, j)),
        out_shape=jax.ShapeDtypeStruct((M, N), out_dtype),
        scratch_shapes=[pltpu.VMEM((tm, D), BF16)],
        compiler_params=_params(2),
    )(*args)


def _mm_res_kernel(*refs, n_a, stream):
    a_refs, w_refs = refs[:n_a], refs[n_a:2 * n_a]
    x_ref, gate_ref, o_ref = refs[2 * n_a:]
    tm = x_ref.shape[0]
    y = jnp.dot(a_refs[0][...], w_refs[0][...], preferred_element_type=F32)
    for a_ref, w_ref in zip(a_refs[1:], w_refs[1:]):
        y = y + jnp.dot(a_ref[...], w_ref[...], preferred_element_type=F32)
    gate = gate_ref[pl.ds(stream.mod_row(pl.program_id(0) * tm), 1), :]
    o_ref[...] = x_ref[...] + gate * y


def matmul_residual(a_list, w, layer, x, ada, gate_col, stream, *, tm=1024, tn=512):
    M, D = a_list[0].shape[0], x.shape[1]
    n_a = len(a_list)
    ka = a_list[0].shape[1]
    assert all(a.shape == (M, ka) for a in a_list) and w.shape[1:] == (n_a * ka, D)
    tm, tn = stream.row_tile(tm), _tile(D, tn)
    gate_block0 = gate_col * (D // tn)
    kernel = functools.partial(_mm_res_kernel, n_a=n_a, stream=stream)
    a_specs = [pl.BlockSpec((tm, ka), lambda i, j: (i, 0)) for _ in a_list]
    w_specs = [pl.BlockSpec((None, ka, tn), functools.partial(lambda i, j, r: (layer, r, j), r=r)) for r in range(n_a)]
    return pl.pallas_call(
        kernel,
        grid=(M // tm, D // tn),
        in_specs=a_specs + w_specs + [pl.BlockSpec((tm, tn), lambda i, j: (i, j)),
                                      pl.BlockSpec((ADA_ROWS, tn), lambda i, j: (0, gate_block0 + j))],
        out_specs=pl.BlockSpec((tm, tn), lambda i, j: (i, j)),
        out_shape=jax.ShapeDtypeStruct((M, D), F32),
        compiler_params=_params(2),
    )(*a_list, *([w] * n_a), x, ada)


def _lmm_kernel(m_ref, u_ref, o_ref):
    o_ref[...] = jnp.dot(m_ref[...], u_ref[...].astype(BF16), preferred_element_type=F32)


def left_matmul(mat, u, *, tm=512, tn=512):
    H, R, K = mat.shape
    B, _, C = u.shape
    tm, tn = _tile(R, tm), _tile(C, tn)
    tiles = R // tm
    return pl.pallas_call(
        _lmm_kernel,
        grid=(B, C // tn, H * tiles),
        in_specs=[pl.BlockSpec((None, tm, K), lambda b, j, i: (i // tiles, i % tiles, 0)),
                  pl.BlockSpec((None, K, tn), lambda b, j, i: (b, 0, j))],
        out_specs=pl.BlockSpec((None, tm, tn), lambda b, j, i: (b, i, j)),
        out_shape=jax.ShapeDtypeStruct((B, H * R, C), F32),
        compiler_params=_params(3),
    )(mat, u)


def dft_matrices(L):
    n = 2 * L
    step = math.gcd(L, 64)
    freq = jnp.arange(L, dtype=jnp.int32)[:, None]

    def table(times):
        ang = ((freq * times[None, :]) % n).astype(F32) * (2.0 * math.pi / n)
        return jnp.cos(ang), jnp.sin(ang)

    (ca, sa), (cb, sb) = table(step * jnp.arange(L // step, dtype=jnp.int32)), table(jnp.arange(step, dtype=jnp.int32))
    ca, sa, cb, sb = ca[:, :, None], sa[:, :, None], cb[:, None, :], sb[:, None, :]
    re = (ca * cb - sa * sb).reshape(L, L)
    sin = (sa * cb + ca * sb).reshape(L, L)
    f = lax.broadcasted_iota(jnp.int32, (L, L), 0)
    t = lax.broadcasted_iota(jnp.int32, (L, L), 1)
    im = jnp.where(f == 0, (1 - 2 * (t % 2)).astype(F32), -sin)
    weight = jnp.where(jnp.arange(L) == 0, 1.0 / n, 2.0 / n)[None, :]
    fwd = jnp.stack([re, im]).astype(BF16)
    inv = jnp.stack([re * weight, im.T * weight]).astype(BF16)
    return fwd, inv


def hyena_filters(L, w1, b1, w2, b2, w3, b3, w4, width):
    hp = lax.Precision.HIGHEST
    t = jnp.linspace(0.0, 1.0, L, dtype=F32)[:, None]
    w = 2.0 * math.pi * jnp.arange(L, dtype=F32)[:, None] / L
    f = jnp.linspace(1e-4, HY_BANDS - 1, HY_BANDS, dtype=F32)[None]
    z = jnp.concatenate([t, jnp.cos(f * w), -jnp.sin(f * w)], axis=-1)
    h = jnp.sin(jnp.dot(z, w1, precision=hp) + b1)
    h = jnp.sin(jnp.dot(h, w2, precision=hp) + b2)
    h = jnp.sin(jnp.dot(h, w3, precision=hp) + b3)
    h = jnp.dot(h, w4, precision=hp).reshape(L, HY_ORDER, 2, width)
    max_decay = math.log(HY_DECAY_TARGET) / HY_FAST_DECAY
    min_decay = math.log(HY_DECAY_TARGET) / HY_SLOW_DECAY
    deltas = jnp.abs(jnp.linspace(min_decay, max_decay, HY_ORDER * width, dtype=F32)).reshape(HY_ORDER, 1, width)
    h = h * jnp.exp(-t[:, :, None, None] * deltas)
    fwd, bwd = h[:, :, 0], h[:, :, 1]
    bwd = bwd.at[0].set(0.0)
    scale = lax.rsqrt(jnp.sum(fwd * fwd, axis=0) + jnp.sum(bwd * bwd, axis=0) + EPS)
    return (fwd * scale).reshape(L, -1), (bwd * scale).reshape(L, -1)


def filter_spectrum(L, filt, width, dft_fwd):
    fwd, bwd = hyena_filters(L, *filt, width)
    spec = left_matmul(dft_fwd, jnp.stack([fwd, bwd]))
    r = lax.broadcasted_iota(jnp.int32, (2 * L, 1), 0)
    return (spec[0] + jnp.where(r <= L, 1.0, -1.0) * spec[1]).reshape(2, L, -1)


def _dft_mul_kernel(m_ref, u_ref, k_ref, o_ref):
    i = pl.program_id(2)
    tm = o_ref.shape[1]
    u = u_ref[...]
    za = jnp.dot(m_ref[0], u, preferred_element_type=F32)
    zb = jnp.dot(m_ref[1], u, preferred_element_type=F32)
    ka, kb = k_ref[0], k_ref[1]
    first = (i * tm + lax.broadcasted_iota(jnp.int32, za.shape, 0)) == 0
    bb = zb * kb
    o_ref[0] = (za * ka - jnp.where(first, 0.0, bb)).astype(o_ref.dtype)
    o_ref[1] = jnp.where(first, bb, za * kb + zb * ka).astype(o_ref.dtype)


def dft_multiply(dft_fwd, u, u_col0, k_spec, k_col0, n_seq, width, *, tm=512, tn=512):
    L = dft_fwd.shape[1]
    tm, tn = _tile(L, tm), _tile(width, tn)
    assert u_col0 % tn == 0 and k_col0 % tn == 0 and u.shape[0] == n_seq * L
    ucb, kcb = u_col0 // tn, k_col0 // tn
    return pl.pallas_call(
        _dft_mul_kernel,
        grid=(n_seq, width // tn, L // tm),
        in_specs=[pl.BlockSpec((2, tm, L), lambda b, j, i: (0, i, 0)),
                  pl.BlockSpec((L, tn), lambda b, j, i: (b, ucb + j)),
                  pl.BlockSpec((2, tm, tn), lambda b, j, i: (0, i, kcb + j))],
        out_specs=pl.BlockSpec((None, 2, tm, tn), lambda b, j, i: (b, 0, i, j)),
        out_shape=jax.ShapeDtypeStruct((n_seq, 2, L, width), BF16),
        compiler_params=_params(3),
    )(dft_fwd, u, k_spec)


def _idft_gate_kernel(m_ref, y_ref, u_ref, skip_ref, gate_ref, o_ref):
    y = (jnp.dot(m_ref[0], y_ref[0], preferred_element_type=F32)
         + jnp.dot(m_ref[1], y_ref[1], preferred_element_type=F32))
    y = y + u_ref[...].astype(F32) * skip_ref[...]
    o_ref[...] = (gate_ref[...].astype(F32) * y).astype(o_ref.dtype)


def idft_gate(dft_inv, y_spec, u, u_col0, skip, skip_row, gate, gate_col0, *, tm=512, tn=512):
    L = dft_inv.shape[1]
    n_seq, _, _, width = y_spec.shape
    tm, tn = _tile(L, tm), _tile(width, tn)
    assert u_col0 % tn == 0 and gate_col0 % tn == 0
    ucb, gcb, tiles = u_col0 // tn, gate_col0 // tn, L // tm
    return pl.pallas_call(
        _idft_gate_kernel,
        grid=(n_seq, width // tn, tiles),
        in_specs=[pl.BlockSpec((2, tm, L), lambda b, j, i: (0, i, 0)),
                  pl.BlockSpec((None, 2, L, tn), lambda b, j, i: (b, 0, 0, j)),
                  pl.BlockSpec((tm, tn), lambda b, j, i: (b * tiles + i, ucb + j)),
                  pl.BlockSpec((None, 1, tn), lambda b, j, i: (skip_row, 0, j)),
                  pl.BlockSpec((tm, tn), lambda b, j, i: (b * tiles + i, gcb + j))],
        out_specs=pl.BlockSpec((tm, tn), lambda b, j, i: (b * tiles + i, j)),
        out_shape=jax.ShapeDtypeStruct((n_seq * L, width), BF16),
        compiler_params=_params(3),
    )(dft_inv, y_spec, u, skip, gate)


def _short_conv_kernel(p_ref, w_ref, b_ref, o_ref):
    x = p_ref[...].astype(F32)
    n = x.shape[0]
    row = lax.broadcasted_iota(jnp.int32, x.shape, 0)
    prev = jnp.where(row == 0, 0.0, pltpu.roll(x, 1, 0))
    nxt = jnp.where(row == n - 1, 0.0, pltpu.roll(x, n - 1, 0))
    o_ref[...] = (prev * w_ref[0:1, :] + x * w_ref[1:2, :] + nxt * w_ref[2:3, :] + b_ref[...]).astype(o_ref.dtype)


def short_conv3(p, row0, n_seq, L, width, w, b, layer, *, tc=256):
    tc = _tile(width, tc)
    assert row0 % L == 0
    rb0 = row0 // L
    return pl.pallas_call(
        _short_conv_kernel,
        grid=(n_seq, width // tc),
        in_specs=[pl.BlockSpec((L, tc), lambda s, j: (rb0 + s, j)),
                  pl.BlockSpec((None, 3, tc), lambda s, j: (layer, 0, j)),
                  pl.BlockSpec((None, 1, tc), lambda s, j: (layer, 0, j))],
        out_specs=pl.BlockSpec((L, tc), lambda s, j: (s, j)),
        out_shape=jax.ShapeDtypeStruct((n_seq * L, width), BF16),
        compiler_params=_params(2),
    )(p, w, b.reshape(b.shape[0], 1, width))


def hyena_mixer(p, row0, n_seq, L, width, conv_w, conv_b, filt, skip, layer):
    vxx = short_conv3(p, row0, n_seq, L, (HY_ORDER + 1) * width, conv_w, conv_b, layer)
    dft_fwd, dft_inv = dft_matrices(L)
    k_spec = filter_spectrum(L, filt, width, dft_fwd)
    skip = skip.reshape(-1, 1, width)
    spec = dft_multiply(dft_fwd, vxx, 0, k_spec, 0, n_seq, width)
    z = idft_gate(dft_inv, spec, vxx, 0, skip, layer * HY_ORDER, vxx, width)
    spec = dft_multiply(dft_fwd, z, 0, k_spec, width, n_seq, width)
    return idft_gate(dft_inv, spec, z, 0, skip, layer * HY_ORDER + 1, vxx, 2 * width)


def _gmlp_kernel(u_ref, v_ref, g_ref, ws_ref, bs_ref, o_ref):
    v = v_ref[...].astype(F32)
    vc = v - jnp.mean(v, axis=-1, keepdims=True)
    vn = vc * lax.rsqrt(jnp.mean(vc * vc, axis=-1, keepdims=True) + EPS) * g_ref[...]
    vn = vn.astype(BF16)
    gd = v.shape[1] // GM_GROUPS
    for g in range(GM_GROUPS):
        cols = slice(g * gd, (g + 1) * gd)
        s = jnp.dot(ws_ref[g].astype(BF16), vn[:, cols], preferred_element_type=F32)
        o_ref[:, cols] = (u_ref[:, cols].astype(F32) * (s + bs_ref[:, cols])).astype(o_ref.dtype)


def chunk_gmlp(p, col0, width, ln_g, ws, bs, layer):
    M = p.shape[0]
    assert col0 % width == 0 and M % GM_CHUNK == 0
    cb = col0 // width
    bs_full = jnp.repeat(bs[layer].T, width // GM_GROUPS, axis=1)
    return pl.pallas_call(
        _gmlp_kernel,
        grid=(M // GM_CHUNK,),
        in_specs=[pl.BlockSpec((GM_CHUNK, width), lambda n: (n, cb)),
                  pl.BlockSpec((GM_CHUNK, width), lambda n: (n, cb + 1)),
                  pl.BlockSpec((None, 1, width), lambda n: (layer, 0, 0)),
                  pl.BlockSpec((None, GM_GROUPS, GM_CHUNK, GM_CHUNK), lambda n: (layer, 0, 0, 0)),
                  pl.BlockSpec((GM_CHUNK, width), lambda n: (0, 0))],
        out_specs=pl.BlockSpec((GM_CHUNK, width), lambda n: (n, 0)),
        out_shape=jax.ShapeDtypeStruct((M, width), BF16),
        compiler_params=_params(1),
    )(p, p, ln_g.reshape(ln_g.shape[0], 1, width), ws, bs_full)


def _attend(lam, q, kv_refs, g_ref, o_ref, out_scale):
    d = q.shape[1] // 2
    maps = []
    for m in range(2):
        cols = slice(m * d, (m + 1) * d)
        ss = [lax.dot_general(q[:, cols], k_ref[:, cols], (((1,), (1,)), ((), ())), preferred_element_type=F32)
              for k_ref, _ in kv_refs]
        mx = functools.reduce(jnp.maximum, [jnp.max(s, axis=-1, keepdims=True) for s in ss])
        es = [jnp.exp2(s - mx) for s in ss]
        total = functools.reduce(jnp.add, [jnp.sum(e, axis=-1, keepdims=True) for e in es])
        maps.append((es, total))
    (e0, l0), (e1, l1) = maps
    r0, r1 = 1.0 / l0, lam / l1
    o = None
    for piece, (_, v_ref) in enumerate(kv_refs):
        a = (e0[piece] * r0 - e1[piece] * r1).astype(BF16)
        pv = jnp.dot(a, v_ref[...], preferred_element_type=F32)
        o = pv if o is None else o + pv
    o = o * lax.rsqrt(jnp.mean(o * o, axis=-1, keepdims=True) + EPS)
    o_ref[...] = (o * (g_ref[...] * out_scale)).astype(o_ref.dtype)


def _attn_kernel(lam_ref, q_ref, kl_ref, vl_ref, kc_ref, vc_ref, g_ref, o_ref, *, n_lat_q, out_scale):
    i = pl.program_id(2)
    lam = lam_ref[0]

    @pl.when(i < n_lat_q)
    def _():
        _attend(lam, q_ref[...], [(kl_ref, vl_ref), (kc_ref, vc_ref)], g_ref, o_ref, out_scale)

    @pl.when(i >= n_lat_q)
    def _():
        _attend(lam, q_ref[...], [(kc_ref, vc_ref)], g_ref, o_ref, out_scale)


def diff_attention(qkv, lam, subln_g, layer, out_scale, stream, *, ctx_queries=True, tq=256):
    M = qkv.shape[0] if ctx_queries else stream.n_lat
    B, L, n_ctx = stream.B, stream.L, stream.n_ctx
    hw = 2 * DA_HEAD_DIM
    W = DA_HEADS * hw
    assert qkv.shape[1] == 3 * W
    tq = _tile(math.gcd(L, n_ctx), tq, 8)
    n_lat_q, n_ctx_q = L // tq, (n_ctx // tq if ctx_queries else 0)
    assert (B * L) % n_ctx == 0
    ctx_block0 = (B * L) // n_ctx

    def q_index(b, h, i):
        return (jnp.where(i < n_lat_q, b * n_lat_q + i, B * n_lat_q + b * n_ctx_q + (i - n_lat_q)), h)

    kernel = functools.partial(_attn_kernel, n_lat_q=n_lat_q, out_scale=out_scale)
    return pl.pallas_call(
        kernel,
        grid=(B, DA_HEADS, n_lat_q + n_ctx_q),
        in_specs=[pl.BlockSpec(memory_space=pltpu.SMEM),
                  pl.BlockSpec((tq, hw), q_index),
                  pl.BlockSpec((L, hw), lambda b, h, i: (b, DA_HEADS + h)),
                  pl.BlockSpec((L, hw), lambda b, h, i: (b, 2 * DA_HEADS + h)),
                  pl.BlockSpec((n_ctx, hw), lambda b, h, i: (ctx_block0 + b, DA_HEADS + h)),
                  pl.BlockSpec((n_ctx, hw), lambda b, h, i: (ctx_block0 + b, 2 * DA_HEADS + h)),
                  pl.BlockSpec((None, 1, hw), lambda b, h, i: (layer, 0, 0))],
        out_specs=pl.BlockSpec((tq, hw), q_index),
        out_shape=jax.ShapeDtypeStruct((M, W), BF16),
        compiler_params=_params(3),
    )(lam, qkv, qkv, qkv, qkv, qkv, subln_g.reshape(subln_g.shape[0], 1, hw))


def _router_kernel(x_ref, ada_ref, g_ref, wr_ref, br_ref, h_ref, ti_ref, tg_ref, *, stream, shift_col, scale_col):
    tm = x_ref.shape[0]
    n_exp = wr_ref.shape[1]
    row = stream.mod_row(pl.program_id(0) * tm)
    h = _modulated_norm(x_ref[...], g_ref[...], ada_ref, row, shift_col, scale_col)
    half = h.shape[1] // 2
    bits = pltpu.bitcast(h.astype(BF16).astype(F32), jnp.uint32)
    h_ref[...] = (bits[:, :half] >> 16) | (bits[:, half:] & jnp.uint32(0xFFFF0000))
    logits = jnp.dot(h, wr_ref[...], precision=lax.Precision.HIGHEST, preferred_element_type=F32) + br_ref[...]

    lane = lax.broadcasted_iota(jnp.int32, logits.shape, 1)
    out_lane = lax.broadcasted_iota(jnp.int32, ti_ref.shape, 1)
    top_i = jnp.zeros(ti_ref.shape, jnp.int32)
    top_v = []
    vals = logits
    for k in range(TOP_K):
        m = jnp.max(vals, axis=-1, keepdims=True)
        sel = jnp.min(jnp.where(vals == m, lane, n_exp), axis=-1, keepdims=True)
        top_i = jnp.where(out_lane == k, sel, top_i)
        top_v.append(m)
        vals = jnp.where(lane == sel, -jnp.inf, vals)
    ti_ref[...] = top_i

    es = [jnp.exp(v - top_v[0]) for v in top_v]
    total = functools.reduce(jnp.add, es)
    gates = jnp.zeros(tg_ref.shape, F32)
    for k in range(TOP_K):
        gates = jnp.where(out_lane == k, es[k] / total, gates)
    tg_ref[...] = gates


def route(x, n_rows, ada, g, wr, br, layer, stream, *, shift_col, scale_col, tm=256):
    D = x.shape[1]
    E = wr.shape[2]
    tm = stream.row_tile(tm)
    assert n_rows % tm == 0
    kernel = functools.partial(_router_kernel, stream=stream, shift_col=shift_col, scale_col=scale_col)
    h, top_i, gates = pl.pallas_call(
        kernel,
        grid=(n_rows // tm,),
        in_specs=[pl.BlockSpec((tm, D), lambda i: (i, 0)),
                  pl.BlockSpec(ada.shape, lambda i: (0, 0)),
                  pl.BlockSpec((1, D), lambda i: (0, 0)),
                  pl.BlockSpec((None, D, E), lambda i: (layer, 0, 0)),
                  pl.BlockSpec((None, 1, E), lambda i: (layer, 0, 0))],
        out_specs=[pl.BlockSpec((tm, D // 2), lambda i: (i, 0)),
                   pl.BlockSpec((tm, LANES), lambda i: (i, 0)),
                   pl.BlockSpec((tm, LANES), lambda i: (i, 0))],
        out_shape=[jax.ShapeDtypeStruct((n_rows, D // 2), jnp.uint32),
                   jax.ShapeDtypeStruct((n_rows, LANES), jnp.int32),
                   jax.ShapeDtypeStruct((n_rows, LANES), F32)],
        compiler_params=_params(1),
    )(x, ada, g.reshape(1, D), wr, br.reshape(br.shape[0], 1, E))
    return h, top_i[:, :TOP_K], gates


def _moe_kernel(te_ref, nt_ref, tok_ref, h_ref, w1_ref, b1_ref, w2_ref, b2_ref, o_ref,
                x_buf, x_sem, w1_scr, w2_scr, *, n_chunks, n_tiles):
    t = pl.program_id(0)
    tm = o_ref.shape[0]
    ff = w2_ref.shape[0]
    fc = ff // n_chunks
    slot = t % 2

    def start_gather(tile, dst_slot):
        for r in range(tm):
            tok = tok_ref[tile * tm + r]
            pltpu.make_async_copy(h_ref.at[tok], x_buf.at[dst_slot, r], x_sem.at[dst_slot]).start()

    def wait_gather(dst_slot):
        pltpu.make_async_copy(h_ref.at[pl.ds(0, tm)], x_buf.at[dst_slot], x_sem.at[dst_slot]).wait()

    @pl.when(t == 0)
    def _():
        start_gather(0, 0)

    new_expert = jnp.logical_or(t == 0, te_ref[t] != te_ref[jnp.maximum(t - 1, 0)])

    used = t < nt_ref[0]

    @pl.when(jnp.logical_and(used, new_expert))
    def _():
        for c in range(2 * n_chunks):
            cols = slice(c * fc, (c + 1) * fc)
            w1_scr[:, cols] = w1_ref[:, cols].astype(BF16)
        for c in range(n_chunks):
            rows = slice(c * fc, (c + 1) * fc)
            w2_scr[rows, :] = w2_ref[rows, :].astype(BF16)

    wait_gather(slot)
    start_gather(jnp.minimum(t + 1, n_tiles - 1), 1 - slot)

    @pl.when(used)
    def _():
        words = jnp.concatenate([x_buf[slot, :, s, :] for s in range(x_buf.shape[2])], axis=-1)
        half = words.shape[1]
        x_lo = pltpu.bitcast(words << 16, F32).astype(BF16)
        x_hi = pltpu.bitcast(words & jnp.uint32(0xFFFF0000), F32).astype(BF16)

        def x_dot(w_cols):
            return (jnp.dot(x_lo, w1_scr[:half, w_cols], preferred_element_type=F32)
                    + jnp.dot(x_hi, w1_scr[half:, w_cols], preferred_element_type=F32))

        y = jnp.zeros(o_ref.shape, F32)
        for c in range(n_chunks):
            glu_cols = slice(c * fc, (c + 1) * fc)
            lin_cols = slice(ff + c * fc, ff + (c + 1) * fc)
            glu = x_dot(glu_cols) + b1_ref[:, glu_cols]
            lin = x_dot(lin_cols) + b1_ref[:, lin_cols]
            glu = jnp.minimum(glu, SWIGLU_LIMIT)
            lin = jnp.clip(lin, -SWIGLU_LIMIT, SWIGLU_LIMIT)
            act = glu * jax.nn.sigmoid(SWIGLU_ALPHA * glu) * (lin + 1.0)
            y = y + jnp.dot(act.astype(BF16), w2_scr[glu_cols, :], preferred_element_type=F32)
        o_ref[...] = (y + b2_ref[...]).astype(o_ref.dtype)

    @pl.when(jnp.logical_not(used))
    def _():
        o_ref[...] = jnp.zeros(o_ref.shape, o_ref.dtype)

    @pl.when(t == n_tiles - 1)
    def _():
        wait_gather(1 - slot)


def moe_experts(h, row_token, tile_expert, n_tiles_used, w1, b1, w2, b2, layer, *, tm):
    N = h.shape[0]
    D = w1.shape[2]
    P = row_token.shape[0]
    assert h.shape[1] * 2 == D and h.shape[1] % LANES == 0
    h = h.reshape(N, h.shape[1] // LANES, LANES)
    E, _, F2 = w1.shape[1:]
    ff = F2 // 2
    n_tiles = P // tm
    kernel = functools.partial(_moe_kernel, n_chunks=3, n_tiles=n_tiles)
    grid_spec = pltpu.PrefetchScalarGridSpec(
        num_scalar_prefetch=3,
        grid=(n_tiles,),
        in_specs=[pl.BlockSpec(memory_space=pl.ANY),
                  pl.BlockSpec((None, None, D, F2), lambda t, te, nt, tok: (layer, te[t], 0, 0)),
                  pl.BlockSpec((None, None, 1, F2), lambda t, te, nt, tok: (layer, te[t], 0, 0)),
                  pl.BlockSpec((None, None, ff, D), lambda t, te, nt, tok: (layer, te[t], 0, 0)),
                  pl.BlockSpec((None, None, 1, D), lambda t, te, nt, tok: (layer, te[t], 0, 0))],
        out_specs=pl.BlockSpec((tm, D), lambda t, te, nt, tok: (t, 0)),
        scratch_shapes=[pltpu.VMEM((2, tm) + h.shape[1:], jnp.uint32), pltpu.SemaphoreType.DMA((2,)),
                        pltpu.VMEM((D, F2), BF16), pltpu.VMEM((ff, D), BF16)],
    )
    return pl.pallas_call(
        kernel,
        grid_spec=grid_spec,
        out_shape=jax.ShapeDtypeStruct((P, D), BF16),
        compiler_params=_params(1),
    )(tile_expert, n_tiles_used, row_token, h, w1, b1.reshape(b1.shape[0], E, 1, F2), w2,
      b2.reshape(b2.shape[0], E, 1, D))


def moe_ffn(h, top_i, w1, b1, w2, b2, layer, *, tm=256):
    N = h.shape[0]
    E = w1.shape[1]
    P = N * TOP_K
    n_tiles = P // tm + E
    i32 = jnp.int32

    pair_e = top_i.reshape(P)
    pair_ids = jnp.arange(P, dtype=i32)
    sorted_e, order = lax.sort((pair_e, pair_ids), num_keys=1, is_stable=True)
    _, rank = lax.sort((order, pair_ids), num_keys=1)
    bounds = jnp.searchsorted(sorted_e, jnp.arange(E + 1, dtype=i32), side="left").astype(i32)
    start_sorted, counts = bounds[:E], bounds[1:] - bounds[:E]
    tiles_per = (counts + tm - 1) // tm
    tile_end = jnp.cumsum(tiles_per)
    tile_start = tile_end - tiles_per
    n_used = tile_end[-1:]
    tile_ids = jnp.minimum(jnp.arange(n_tiles, dtype=i32), n_used - 1)
    tile_expert = jnp.sum((tile_ids[:, None] >= tile_end[None, :]).astype(i32), axis=1)

    all_tiles = jnp.arange(n_tiles, dtype=i32)
    tile_off0 = (all_tiles - tile_start[tile_expert]) * tm
    off = tile_off0[:, None] + jnp.arange(tm, dtype=i32)[None, :]
    valid = (off < counts[tile_expert][:, None]) & (all_tiles < n_used)[:, None]
    sorted_pos = jnp.clip(start_sorted[tile_expert][:, None] + off, 0, P - 1)
    row_pair = order[sorted_pos.reshape(-1)]
    row_token = jnp.where(valid.reshape(-1), row_pair // TOP_K, 0)
    pair_row = (tile_start[pair_e] * tm + rank - start_sorted[pair_e]).reshape(N, TOP_K)

    ys = moe_experts(h, row_token, tile_expert, n_used, w1, b1, w2, b2, layer, tm=tm)
    return [ys[pair_row[:, k]] for k in range(TOP_K)]


def _combine_kernel(x_ref, gate_ref, tg_ref, *refs, stream):
    y_refs, o_ref = refs[:-1], refs[-1]
    tm = x_ref.shape[0]
    gate = gate_ref[pl.ds(stream.mod_row(pl.program_id(0) * tm), 1), :]
    tg = tg_ref[...]
    y = functools.reduce(jnp.add, [tg[:, k:k + 1] * y_ref[...].astype(F32) for k, y_ref in enumerate(y_refs)])
    o_ref[...] = x_ref[...] + gate * y


def combine_residual(x, ys, token_gates, ada, gate_col, stream, *, tm=512):
    n, D = ys[0].shape
    tm = stream.row_tile(tm)
    assert n % tm == 0
    row_spec = pl.BlockSpec((tm, D), lambda i: (i, 0))
    return pl.pallas_call(
        functools.partial(_combine_kernel, stream=stream),
        grid=(n // tm,),
        in_specs=[row_spec, pl.BlockSpec((ADA_ROWS, D), lambda i: (0, gate_col)),
                  pl.BlockSpec((tm, LANES), lambda i: (i, 0))] + [row_spec] * len(ys),
        out_specs=row_spec,
        out_shape=jax.ShapeDtypeStruct((n, D), F32),
        compiler_params=_params(1),
    )(x, ada, token_gates, *ys)


def rms_norm(x, g):
    return x * lax.rsqrt(jnp.mean(x * x, axis=-1, keepdims=True) + EPS) * g


def kernel(x, c, ctx, c_ctx, ada_w, ada_b, norm_g, final_g, ev_w_in, ev_w_out, hy_conv_w, hy_conv_b,
           hy_f_w1, hy_f_b1, hy_f_w2, hy_f_b2, hy_f_w3, hy_f_b3, hy_f_w4, hy_skip, gm_ln_g, gm_ws, gm_bs,
           od_w_qkv, od_w_out, od_lambda, od_subln_g, moe_wr, moe_br, moe_w1, moe_b1, moe_w2, moe_b2):
    B, L, D = x.shape
    n_ctx = ctx.shape[1]
    depth = ada_w.shape[0]
    hy_width = hy_skip.shape[2]
    hy_split = (HY_ORDER + 1) * hy_width
    gm_width = gm_ln_g.shape[1]
    stream = Stream(B, L, n_ctx)
    n_lat, M = stream.n_lat, stream.rows

    xs = jnp.concatenate([x.reshape(n_lat, D), ctx.reshape(B * n_ctx, D)], axis=0)
    cond = jnp.concatenate([jax.nn.silu(c), jax.nn.silu(c_ctx)[None], jnp.zeros((ADA_ROWS - B - 1, D), F32)], axis=0)
    ev_w_in, ev_w_out, od_w_qkv, od_w_out = (w.astype(BF16) for w in (ev_w_in, ev_w_out, od_w_qkv, od_w_out))

    for l in range(depth):
        i = l // 2
        last = l == depth - 1
        ada = matmul(cond.astype(BF16), ada_w, l, tm=ADA_ROWS, tn=1024) + ada_b[l]

        if l % 2 == 0:
            p = norm_matmul(xs, ada, norm_g[l, 0], ev_w_in, i, stream, shift_col=0, scale_col=1, out_dtype=BF16)
            filt = (hy_f_w1[i], hy_f_b1[i], hy_f_w2[i], hy_f_b2[i], hy_f_w3[i], hy_f_b3[i], hy_f_w4[i])
            y_a = jnp.concatenate(
                [hyena_mixer(p, 0, B, L, hy_width, hy_conv_w, hy_conv_b, filt, hy_skip, i),
                 hyena_mixer(p, n_lat, B, n_ctx, hy_width, hy_conv_w, hy_conv_b, filt, hy_skip, i)], axis=0)
            y_b = chunk_gmlp(p, hy_split, gm_width, gm_ln_g, gm_ws, gm_bs, i)
            xs = matmul_residual([y_a, y_b], ev_w_out, i, xs, ada, 2, stream)
        else:
            lam_init = 0.8 - 0.6 * math.exp(-0.3 * l)
            W = od_w_qkv.shape[2] // 3
            qkv = norm_matmul(xs, ada, norm_g[l, 0], od_w_qkv, i, stream, shift_col=0, scale_col=1, out_dtype=BF16,
                              rope_cols=(2 * W, W, math.log2(math.e) / math.sqrt(DA_HEAD_DIM)))
            lp = od_lambda[i]
            lam = (jnp.exp(jnp.sum(lp[0] * lp[1])) - jnp.exp(jnp.sum(lp[2] * lp[3])) + lam_init).reshape(1)
            o = diff_attention(qkv, lam, od_subln_g, i, 1.0 - lam_init, stream, ctx_queries=not last)
            xs = matmul_residual([o], od_w_out, i, xs, ada, 2, stream)

        n_tok = n_lat if last else M
        h, top_i, gates = route(xs, n_tok, ada, norm_g[l, 1], moe_wr, moe_br, l, stream, shift_col=3, scale_col=4)
        outs = moe_ffn(h, top_i, moe_w1, moe_b1, moe_w2, moe_b2, l)
        xs = combine_residual(xs, outs, gates, ada, 5, stream)

    return rms_norm(xs[:n_lat].reshape(B, L, D), final_g)
```

```python
import functools
import math

import jax
import jax.numpy as jnp
from jax import lax
from jax.experimental import pallas as pl
from jax.experimental.pallas import tpu as pltpu

F32 = jnp.float32
BF16 = jnp.bfloat16

EPS = 1e-6
GRID_W = 64

HY_ORDER = 2
HY_EMB = 33
HY_BANDS = (HY_EMB - 1) // 2
HY_DECAY_TARGET = 1e-2
HY_FAST_DECAY = 0.3
HY_SLOW_DECAY = 1.5

GM_GROUPS = 8
GM_CHUNK = 128

DA_HEADS = 8
DA_HEAD_DIM = 128
ROPE_AXIS_DIM = DA_HEAD_DIM // 2
ROPE_THETA = 10000.0

TOP_K = 4
SWIGLU_ALPHA = 1.702
SWIGLU_LIMIT = 7.0

LANES = 128
MIB = 1024 * 1024
VMEM_LIMIT_BYTES = 58 * MIB
ADA_ROWS = 8


def _params(n_grid_dims, vmem=VMEM_LIMIT_BYTES):
    return pltpu.CompilerParams(dimension_semantics=("arbitrary",) * n_grid_dims, vmem_limit_bytes=vmem)


def _tile(n, want, align=LANES):
    if n <= want:
        return n
    return next(t for t in range(want - want % align, 0, -align) if n % t == 0)


class Stream:
    def __init__(self, B, L, n_ctx):
        assert B + 1 <= ADA_ROWS
        self.B, self.L, self.n_ctx = B, L, n_ctx
        self.n_lat = B * L
        self.rows = B * L + B * n_ctx

    def row_tile(self, want):
        return _tile(math.gcd(self.L, self.B * self.n_ctx), want)

    def mod_row(self, row0):
        return jnp.where(row0 < self.n_lat, row0 // self.L, self.B)


def _mm_kernel(a_ref, w_ref, o_ref):
    o_ref[...] = jnp.dot(a_ref[...].astype(BF16), w_ref[...].astype(BF16),
                         preferred_element_type=F32).astype(o_ref.dtype)


def matmul(a, w, layer, *, tm=512, tn=512):
    M, K = a.shape
    N = w.shape[2]
    tm, tn = _tile(M, tm, 8), _tile(N, tn)
    return pl.pallas_call(
        _mm_kernel,
        grid=(N // tn, M // tm),
        in_specs=[pl.BlockSpec((tm, K), lambda j, i: (i, 0)),
                  pl.BlockSpec((None, K, tn), lambda j, i: (layer, 0, j))],
        out_specs=pl.BlockSpec((tm, tn), lambda j, i: (i, j)),
        out_shape=jax.ShapeDtypeStruct((M, N), F32),
        compiler_params=_params(2),
    )(a, w)


def _modulated_norm(x, g, ada_ref, row, shift_col, scale_col):
    D = x.shape[1]
    y = x * lax.rsqrt(jnp.mean(x * x, axis=-1, keepdims=True) + EPS) * g
    scale = ada_ref[pl.ds(row, 1), scale_col * D:(scale_col + 1) * D]
    shift = ada_ref[pl.ds(row, 1), shift_col * D:(shift_col + 1) * D]
    return y * (1.0 + scale) + shift


def _norm_mm_kernel(*refs, stream, shift_col, scale_col, rope):
    if rope is None:
        x_ref, ada_ref, g_ref, w_ref, o_ref, h_scr = refs
    else:
        x_ref, ada_ref, g_ref, w_ref, cos_ref, sin_ref, o_ref, h_scr = refs
    i, j = pl.program_id(0), pl.program_id(1)
    tm, tn = o_ref.shape

    @pl.when(j == 0)
    def _():
        h = _modulated_norm(x_ref[...], g_ref[...], ada_ref, stream.mod_row(i * tm), shift_col, scale_col)
        h_scr[...] = h.astype(BF16)

    y = jnp.dot(h_scr[...], w_ref[...], preferred_element_type=F32)
    if rope is None:
        o_ref[...] = y.astype(o_ref.dtype)
        return

    n_rope_tiles, n_q_tiles, q_scale = rope

    @pl.when(j < n_rope_tiles)
    def _():
        scale = jnp.where(j < n_q_tiles, q_scale, 1.0)
        lane = lax.broadcasted_iota(jnp.int32, (tm, DA_HEAD_DIM), 1)
        first_half = (lane % ROPE_AXIS_DIM) < ROPE_AXIS_DIM // 2
        for hh in range(tn // DA_HEAD_DIM):
            cols = slice(hh * DA_HEAD_DIM, (hh + 1) * DA_HEAD_DIM)
            seg = y[:, cols]
            partner = jnp.where(first_half,
                                pltpu.roll(seg, DA_HEAD_DIM - ROPE_AXIS_DIM // 2, 1),
                                pltpu.roll(seg, ROPE_AXIS_DIM // 2, 1))
            o_ref[:, cols] = ((seg * cos_ref[...] + partner * sin_ref[...]) * scale).astype(o_ref.dtype)

    @pl.when(j >= n_rope_tiles)
    def _():
        o_ref[...] = y.astype(o_ref.dtype)


def rope_tables(L, tile_rows):
    pos = jnp.arange(L)
    inv = ROPE_THETA ** (-jnp.arange(0, ROPE_AXIS_DIM, 2, dtype=F32) / ROPE_AXIS_DIM)
    ar = (pos // GRID_W).astype(F32)[:, None] * inv
    ac = (pos % GRID_W).astype(F32)[:, None] * inv
    cos = jnp.concatenate([jnp.cos(ar), jnp.cos(ar), jnp.cos(ac), jnp.cos(ac)], axis=-1)
    sin = jnp.concatenate([-jnp.sin(ar), jnp.sin(ar), -jnp.sin(ac), jnp.sin(ac)], axis=-1)
    cos = jnp.concatenate([cos, jnp.ones((tile_rows, DA_HEAD_DIM), F32)], axis=0)
    sin = jnp.concatenate([sin, jnp.zeros((tile_rows, DA_HEAD_DIM), F32)], axis=0)
    return cos, sin


def norm_matmul(x, ada, g, w, layer, stream, *, shift_col, scale_col, out_dtype, rope_cols=None, tm=1024, tn=512):
    M, D = x.shape
    N = w.shape[2]
    tm, tn = stream.row_tile(tm), _tile(N, tn)
    in_specs = [pl.BlockSpec((tm, D), lambda i, j: (i, 0)),
                pl.BlockSpec(ada.shape, lambda i, j: (0, 0)),
                pl.BlockSpec((1, D), lambda i, j: (0, 0)),
                pl.BlockSpec((None, D, tn), lambda i, j: (layer, 0, j))]
    args = [x, ada, g.reshape(1, D), w]
    rope = None
    if rope_cols is not None:
        n_rope, n_q, q_scale = rope_cols
        assert n_rope % tn == 0 and n_q % tn == 0 and tn % DA_HEAD_DIM == 0
        rope = (n_rope // tn, n_q // tn, q_scale)
        lat_tiles, tiles_per_sample = stream.n_lat // tm, stream.L // tm
        table_spec = pl.BlockSpec(
            (tm, DA_HEAD_DIM), lambda i, j: (jnp.where(i < lat_tiles, i % tiles_per_sample, tiles_per_sample), 0))
        in_specs += [table_spec, table_spec]
        args += list(rope_tables(stream.L, tm))
    kernel = functools.partial(_norm_mm_kernel, stream=stream, shift_col=shift_col, scale_col=scale_col, rope=rope)
    return pl.pallas_call(
        kernel,
        grid=(M // tm, N // tn),
        in_specs=in_specs,
        out_specs=pl.BlockSpec((tm, tn), lambda i, j: (i, j)),
        out_shape=jax.ShapeDtypeStruct((M, N), out_dtype),
        scratch_shapes=[pltpu.VMEM((tm, D), BF16)],
        compiler_params=_params(2),
    )(*args)


def _mm_res_kernel(*refs, n_a, stream):
    a_refs, w_refs = refs[:n_a], refs[n_a:2 * n_a]
    x_ref, gate_ref, o_ref = refs[2 * n_a:]
    tm = x_ref.shape[0]
    y = jnp.dot(a_refs[0][...], w_refs[0][...], preferred_element_type=F32)
    for a_ref, w_ref in zip(a_refs[1:], w_refs[1:]):
        y = y + jnp.dot(a_ref[...], w_ref[...], preferred_element_type=F32)
    gate = gate_ref[pl.ds(stream.mod_row(pl.program_id(0) * tm), 1), :]
    o_ref[...] = x_ref[...] + gate * y


def matmul_residual(a_list, w, layer, x, ada, gate_col, stream, *, tm=1024, tn=512):
    M, D = a_list[0].shape[0], x.shape[1]
    n_a = len(a_list)
    ka = a_list[0].shape[1]
    assert all(a.shape == (M, ka) for a in a_list) and w.shape[1:] == (n_a * ka, D)
    tm, tn = stream.row_tile(tm), _tile(D, tn)
    gate_block0 = gate_col * (D // tn)
    kernel = functools.partial(_mm_res_kernel, n_a=n_a, stream=stream)
    a_specs = [pl.BlockSpec((tm, ka), lambda i, j: (i, 0)) for _ in a_list]
    w_specs = [pl.BlockSpec((None, ka, tn), functools.partial(lambda i, j, r: (layer, r, j), r=r)) for r in range(n_a)]
    return pl.pallas_call(
        kernel,
        grid=(M // tm, D // tn),
        in_specs=a_specs + w_specs + [pl.BlockSpec((tm, tn), lambda i, j: (i, j)),
                                      pl.BlockSpec((ADA_ROWS, tn), lambda i, j: (0, gate_block0 + j))],
        out_specs=pl.BlockSpec((tm, tn), lambda i, j: (i, j)),
        out_shape=jax.ShapeDtypeStruct((M, D), F32),
        compiler_params=_params(2),
    )(*a_list, *([w] * n_a), x, ada)


def _lmm_kernel(m_ref, u_ref, o_ref):
    o_ref[...] = jnp.dot(m_ref[...], u_ref[...].astype(BF16), preferred_element_type=F32)


def left_matmul(mat, u, *, tm=512, tn=512):
    H, R, K = mat.shape
    B, _, C = u.shape
    tm, tn = _tile(R, tm), _tile(C, tn)
    tiles = R // tm
    return pl.pallas_call(
        _lmm_kernel,
        grid=(B, C // tn, H * tiles),
        in_specs=[pl.BlockSpec((None, tm, K), lambda b, j, i: (i // tiles, i % tiles, 0)),
                  pl.BlockSpec((None, K, tn), lambda b, j, i: (b, 0, j))],
        out_specs=pl.BlockSpec((None, tm, tn), lambda b, j, i: (b, i, j)),
        out_shape=jax.ShapeDtypeStruct((B, H * R, C), F32),
        compiler_params=_params(3),
    )(mat, u)


def dft_matrices(L):
    n = 2 * L
    step = math.gcd(L, 64)
    freq = jnp.arange(L, dtype=jnp.int32)[:, None]

    def table(times):
        ang = ((freq * times[None, :]) % n).astype(F32) * (2.0 * math.pi / n)
        return jnp.cos(ang), jnp.sin(ang)

    (ca, sa), (cb, sb) = table(step * jnp.arange(L // step, dtype=jnp.int32)), table(jnp.arange(step, dtype=jnp.int32))
    ca, sa, cb, sb = ca[:, :, None], sa[:, :, None], cb[:, None, :], sb[:, None, :]
    re = (ca * cb - sa * sb).reshape(L, L)
    sin = (sa * cb + ca * sb).reshape(L, L)
    f = lax.broadcasted_iota(jnp.int32, (L, L), 0)
    t = lax.broadcasted_iota(jnp.int32, (L, L), 1)
    im = jnp.where(f == 0, (1 - 2 * (t % 2)).astype(F32), -sin)
    weight = jnp.where(jnp.arange(L) == 0, 1.0 / n, 2.0 / n)[None, :]
    fwd = jnp.stack([re, im]).astype(BF16)
    inv = jnp.stack([re * weight, im.T * weight]).astype(BF16)
    return fwd, inv


def hyena_filters(L, w1, b1, w2, b2, w3, b3, w4, width):
    hp = lax.Precision.HIGHEST
    t = jnp.linspace(0.0, 1.0, L, dtype=F32)[:, None]
    w = 2.0 * math.pi * jnp.arange(L, dtype=F32)[:, None] / L
    f = jnp.linspace(1e-4, HY_BANDS - 1, HY_BANDS, dtype=F32)[None]
    z = jnp.concatenate([t, jnp.cos(f * w), -jnp.sin(f * w)], axis=-1)
    h = jnp.sin(jnp.dot(z, w1, precision=hp) + b1)
    h = jnp.sin(jnp.dot(h, w2, precision=hp) + b2)
    h = jnp.sin(jnp.dot(h, w3, precision=hp) + b3)
    h = jnp.dot(h, w4, precision=hp).reshape(L, HY_ORDER, 2, width)
    max_decay = math.log(HY_DECAY_TARGET) / HY_FAST_DECAY
    min_decay = math.log(HY_DECAY_TARGET) / HY_SLOW_DECAY
    deltas = jnp.abs(jnp.linspace(min_decay, max_decay, HY_ORDER * width, dtype=F32)).reshape(HY_ORDER, 1, width)
    h = h * jnp.exp(-t[:, :, None, None] * deltas)
    fwd, bwd = h[:, :, 0], h[:, :, 1]
    bwd = bwd.at[0].set(0.0)
    scale = lax.rsqrt(jnp.sum(fwd * fwd, axis=0) + jnp.sum(bwd * bwd, axis=0) + EPS)
    return (fwd * scale).reshape(L, -1), (bwd * scale).reshape(L, -1)


def filter_spectrum(L, filt, width, dft_fwd):
    fwd, bwd = hyena_filters(L, *filt, width)
    spec = left_matmul(dft_fwd, jnp.stack([fwd, bwd]))
    r = lax.broadcasted_iota(jnp.int32, (2 * L, 1), 0)
    return (spec[0] + jnp.where(r <= L, 1.0, -1.0) * spec[1]).reshape(2, L, -1)


def _dft_mul_kernel(m_ref, u_ref, k_ref, o_ref):
    i = pl.program_id(2)
    tm = o_ref.shape[1]
    u = u_ref[...]
    za = jnp.dot(m_ref[0], u, preferred_element_type=F32)
    zb = jnp.dot(m_ref[1], u, preferred_element_type=F32)
    ka, kb = k_ref[0], k_ref[1]
    first = (i * tm + lax.broadcasted_iota(jnp.int32, za.shape, 0)) == 0
    bb = zb * kb
    o_ref[0] = (za * ka - jnp.where(first, 0.0, bb)).astype(o_ref.dtype)
    o_ref[1] = jnp.where(first, bb, za * kb + zb * ka).astype(o_ref.dtype)


def dft_multiply(dft_fwd, u, u_col0, k_spec, k_col0, n_seq, width, *, tm=512, tn=512):
    L = dft_fwd.shape[1]
    tm, tn = _tile(L, tm), _tile(width, tn)
    assert u_col0 % tn == 0 and k_col0 % tn == 0 and u.shape[0] == n_seq * L
    ucb, kcb = u_col0 // tn, k_col0 // tn
    return pl.pallas_call(
        _dft_mul_kernel,
        grid=(n_seq, width // tn, L // tm),
        in_specs=[pl.BlockSpec((2, tm, L), lambda b, j, i: (0, i, 0)),
                  pl.BlockSpec((L, tn), lambda b, j, i: (b, ucb + j)),
                  pl.BlockSpec((2, tm, tn), lambda b, j, i: (0, i, kcb + j))],
        out_specs=pl.BlockSpec((None, 2, tm, tn), lambda b, j, i: (b, 0, i, j)),
        out_shape=jax.ShapeDtypeStruct((n_seq, 2, L, width), BF16),
        compiler_params=_params(3),
    )(dft_fwd, u, k_spec)


def _idft_gate_kernel(m_ref, y_ref, u_ref, skip_ref, gate_ref, o_ref):
    y = (jnp.dot(m_ref[0], y_ref[0], preferred_element_type=F32)
         + jnp.dot(m_ref[1], y_ref[1], preferred_element_type=F32))
    y = y + u_ref[...].astype(F32) * skip_ref[...]
    o_ref[...] = (gate_ref[...].astype(F32) * y).astype(o_ref.dtype)


def idft_gate(dft_inv, y_spec, u, u_col0, skip, skip_row, gate, gate_col0, *, tm=512, tn=512):
    L = dft_inv.shape[1]
    n_seq, _, _, width = y_spec.shape
    tm, tn = _tile(L, tm), _tile(width, tn)
    assert u_col0 % tn == 0 and gate_col0 % tn == 0
    ucb, gcb, tiles = u_col0 // tn, gate_col0 // tn, L // tm
    return pl.pallas_call(
        _idft_gate_kernel,
        grid=(n_seq, width // tn, tiles),
        in_specs=[pl.BlockSpec((2, tm, L), lambda b, j, i: (0, i, 0)),
                  pl.BlockSpec((None, 2, L, tn), lambda b, j, i: (b, 0, 0, j)),
                  pl.BlockSpec((tm, tn), lambda b, j, i: (b * tiles + i, ucb + j)),
                  pl.BlockSpec((None, 1, tn), lambda b, j, i: (skip_row, 0, j)),
                  pl.BlockSpec((tm, tn), lambda b, j, i: (b * tiles + i, gcb + j))],
        out_specs=pl.BlockSpec((tm, tn), lambda b, j, i: (b * tiles + i, j)),
        out_shape=jax.ShapeDtypeStruct((n_seq * L, width), BF16),
        compiler_params=_params(3),
    )(dft_inv, y_spec, u, skip, gate)


def _short_conv_kernel(p_ref, w_ref, b_ref, o_ref):
    x = p_ref[...].astype(F32)
    n = x.shape[0]
    row = lax.broadcasted_iota(jnp.int32, x.shape, 0)
    prev = jnp.where(row == 0, 0.0, pltpu.roll(x, 1, 0))
    nxt = jnp.where(row == n - 1, 0.0, pltpu.roll(x, n - 1, 0))
    o_ref[...] = (prev * w_ref[0:1, :] + x * w_ref[1:2, :] + nxt * w_ref[2:3, :] + b_ref[...]).astype(o_ref.dtype)


def short_conv3(p, row0, n_seq, L, width, w, b, layer, *, tc=256):
    tc = _tile(width, tc)
    assert row0 % L == 0
    rb0 = row0 // L
    return pl.pallas_call(
        _short_conv_kernel,
        grid=(n_seq, width // tc),
        in_specs=[pl.BlockSpec((L, tc), lambda s, j: (rb0 + s, j)),
                  pl.BlockSpec((None, 3, tc), lambda s, j: (layer, 0, j)),
                  pl.BlockSpec((None, 1, tc), lambda s, j: (layer, 0, j))],
        out_specs=pl.BlockSpec((L, tc), lambda s, j: (s, j)),
        out_shape=jax.ShapeDtypeStruct((n_seq * L, width), BF16),
        compiler_params=_params(2),
    )(p, w, b.reshape(b.shape[0], 1, width))


def hyena_mixer(p, row0, n_seq, L, width, conv_w, conv_b, filt, skip, layer):
    vxx = short_conv3(p, row0, n_seq, L, (HY_ORDER + 1) * width, conv_w, conv_b, layer)
    dft_fwd, dft_inv = dft_matrices(L)
    k_spec = filter_spectrum(L, filt, width, dft_fwd)
    skip = skip.reshape(-1, 1, width)
    spec = dft_multiply(dft_fwd, vxx, 0, k_spec, 0, n_seq, width)
    z = idft_gate(dft_inv, spec, vxx, 0, skip, layer * HY_ORDER, vxx, width)
    spec = dft_multiply(dft_fwd, z, 0, k_spec, width, n_seq, width)
    return idft_gate(dft_inv, spec, z, 0, skip, layer * HY_ORDER + 1, vxx, 2 * width)


def _gmlp_kernel(u_ref, v_ref, g_ref, ws_ref, bs_ref, o_ref):
    v = v_ref[...].astype(F32)
    vc = v - jnp.mean(v, axis=-1, keepdims=True)
    vn = vc * lax.rsqrt(jnp.mean(vc * vc, axis=-1, keepdims=True) + EPS) * g_ref[...]
    vn = vn.astype(BF16)
    gd = v.shape[1] // GM_GROUPS
    for g in range(GM_GROUPS):
        cols = slice(g * gd, (g + 1) * gd)
        s = jnp.dot(ws_ref[g].astype(BF16), vn[:, cols], preferred_element_type=F32)
        o_ref[:, cols] = (u_ref[:, cols].astype(F32) * (s + bs_ref[:, cols])).astype(o_ref.dtype)


def chunk_gmlp(p, col0, width, ln_g, ws, bs, layer):
    M = p.shape[0]
    assert col0 % width == 0 and M % GM_CHUNK == 0
    cb = col0 // width
    bs_full = jnp.repeat(bs[layer].T, width // GM_GROUPS, axis=1)
    return pl.pallas_call(
        _gmlp_kernel,
        grid=(M // GM_CHUNK,),
        in_specs=[pl.BlockSpec((GM_CHUNK, width), lambda n: (n, cb)),
                  pl.BlockSpec((GM_CHUNK, width), lambda n: (n, cb + 1)),
                  pl.BlockSpec((None, 1, width), lambda n: (layer, 0, 0)),
                  pl.BlockSpec((None, GM_GROUPS, GM_CHUNK, GM_CHUNK), lambda n: (layer, 0, 0, 0)),
                  pl.BlockSpec((GM_CHUNK, width), lambda n: (0, 0))],
        out_specs=pl.BlockSpec((GM_CHUNK, width), lambda n: (n, 0)),
        out_shape=jax.ShapeDtypeStruct((M, width), BF16),
        compiler_params=_params(1),
    )(p, p, ln_g.reshape(ln_g.shape[0], 1, width), ws, bs_full)


def _attend(lam, q, kv_refs, g_ref, o_ref, out_scale):
    d = q.shape[1] // 2
    maps = []
    for m in range(2):
        cols = slice(m * d, (m + 1) * d)
        ss = [lax.dot_general(q[:, cols], k_ref[:, cols], (((1,), (1,)), ((), ())), preferred_element_type=F32)
              for k_ref, _ in kv_refs]
        mx = functools.reduce(jnp.maximum, [jnp.max(s, axis=-1, keepdims=True) for s in ss])
        es = [jnp.exp2(s - mx) for s in ss]
        total = functools.reduce(jnp.add, [jnp.sum(e, axis=-1, keepdims=True) for e in es])
        maps.append((es, total))
    (e0, l0), (e1, l1) = maps
    r0, r1 = 1.0 / l0, lam / l1
    o = None
    for piece, (_, v_ref) in enumerate(kv_refs):
        a = (e0[piece] * r0 - e1[piece] * r1).astype(BF16)
        pv = jnp.dot(a, v_ref[...], preferred_element_type=F32)
        o = pv if o is None else o + pv
    o = o * lax.rsqrt(jnp.mean(o * o, axis=-1, keepdims=True) + EPS)
    o_ref[...] = (o * (g_ref[...] * out_scale)).astype(o_ref.dtype)


def _attn_kernel(lam_ref, q_ref, kl_ref, vl_ref, kc_ref, vc_ref, g_ref, o_ref, *, n_lat_q, out_scale):
    i = pl.program_id(2)
    lam = lam_ref[0]

    @pl.when(i < n_lat_q)
    def _():
        _attend(lam, q_ref[...], [(kl_ref, vl_ref), (kc_ref, vc_ref)], g_ref, o_ref, out_scale)

    @pl.when(i >= n_lat_q)
    def _():
        _attend(lam, q_ref[...], [(kc_ref, vc_ref)], g_ref, o_ref, out_scale)


def diff_attention(qkv, lam, subln_g, layer, out_scale, stream, *, ctx_queries=True, tq=256):
    M = qkv.shape[0] if ctx_queries else stream.n_lat
    B, L, n_ctx = stream.B, stream.L, stream.n_ctx
    hw = 2 * DA_HEAD_DIM
    W = DA_HEADS * hw
    assert qkv.shape[1] == 3 * W
    tq = _tile(math.gcd(L, n_ctx), tq, 8)
    n_lat_q, n_ctx_q = L // tq, (n_ctx // tq if ctx_queries else 0)
    assert (B * L) % n_ctx == 0
    ctx_block0 = (B * L) // n_ctx

    def q_index(b, h, i):
        return (jnp.where(i < n_lat_q, b * n_lat_q + i, B * n_lat_q + b * n_ctx_q + (i - n_lat_q)), h)

    kernel = functools.partial(_attn_kernel, n_lat_q=n_lat_q, out_scale=out_scale)
    return pl.pallas_call(
        kernel,
        grid=(B, DA_HEADS, n_lat_q + n_ctx_q),
        in_specs=[pl.BlockSpec(memory_space=pltpu.SMEM),
                  pl.BlockSpec((tq, hw), q_index),
                  pl.BlockSpec((L, hw), lambda b, h, i: (b, DA_HEADS + h)),
                  pl.BlockSpec((L, hw), lambda b, h, i: (b, 2 * DA_HEADS + h)),
                  pl.BlockSpec((n_ctx, hw), lambda b, h, i: (ctx_block0 + b, DA_HEADS + h)),
                  pl.BlockSpec((n_ctx, hw), lambda b, h, i: (ctx_block0 + b, 2 * DA_HEADS + h)),
                  pl.BlockSpec((None, 1, hw), lambda b, h, i: (layer, 0, 0))],
        out_specs=pl.BlockSpec((tq, hw), q_index),
        out_shape=jax.ShapeDtypeStruct((M, W), BF16),
        compiler_params=_params(3),
    )(lam, qkv, qkv, qkv, qkv, qkv, subln_g.reshape(subln_g.shape[0], 1, hw))


def _router_kernel(x_ref, ada_ref, g_ref, wr_ref, br_ref, h_ref, ti_ref, tg_ref, *, stream, shift_col, scale_col):
    tm = x_ref.shape[0]
    n_exp = wr_ref.shape[1]
    row = stream.mod_row(pl.program_id(0) * tm)
    h = _modulated_norm(x_ref[...], g_ref[...], ada_ref, row, shift_col, scale_col)
    half = h.shape[1] // 2
    bits = pltpu.bitcast(h.astype(BF16).astype(F32), jnp.uint32)
    h_ref[...] = (bits[:, :half] >> 16) | (bits[:, half:] & jnp.uint32(0xFFFF0000))
    wr = wr_ref[...]
    h_hi, w_hi = h.astype(BF16), wr.astype(BF16)
    h_lo, w_lo = (h - h_hi.astype(F32)).astype(BF16), (wr - w_hi.astype(F32)).astype(BF16)
    logits = (jnp.dot(h_hi, w_hi, preferred_element_type=F32) + jnp.dot(h_hi, w_lo, preferred_element_type=F32)
              + jnp.dot(h_lo, w_hi, preferred_element_type=F32)) + br_ref[...]

    lane = lax.broadcasted_iota(jnp.int32, logits.shape, 1)
    out_lane = lax.broadcasted_iota(jnp.int32, ti_ref.shape, 1)
    top_i = jnp.zeros(ti_ref.shape, jnp.int32)
    top_v = []
    vals = logits
    for k in range(TOP_K):
        m = jnp.max(vals, axis=-1, keepdims=True)
        sel = jnp.min(jnp.where(vals == m, lane, n_exp), axis=-1, keepdims=True)
        top_i = jnp.where(out_lane == k, sel, top_i)
        top_v.append(m)
        vals = jnp.where(lane == sel, -jnp.inf, vals)
    ti_ref[...] = top_i

    es = [jnp.exp(v - top_v[0]) for v in top_v]
    total = functools.reduce(jnp.add, es)
    gates = jnp.zeros(tg_ref.shape, F32)
    for k in range(TOP_K):
        gates = jnp.where(out_lane == k, es[k] / total, gates)
    tg_ref[...] = gates


def route(x, n_rows, ada, g, wr, br, layer, stream, *, shift_col, scale_col, tm=256):
    D = x.shape[1]
    E = wr.shape[2]
    tm = stream.row_tile(tm)
    assert n_rows % tm == 0
    kernel = functools.partial(_router_kernel, stream=stream, shift_col=shift_col, scale_col=scale_col)
    h, top_i, gates = pl.pallas_call(
        kernel,
        grid=(n_rows // tm,),
        in_specs=[pl.BlockSpec((tm, D), lambda i: (i, 0)),
                  pl.BlockSpec(ada.shape, lambda i: (0, 0)),
                  pl.BlockSpec((1, D), lambda i: (0, 0)),
                  pl.BlockSpec((None, D, E), lambda i: (layer, 0, 0)),
                  pl.BlockSpec((None, 1, E), lambda i: (layer, 0, 0))],
        out_specs=[pl.BlockSpec((tm, D // 2), lambda i: (i, 0)),
                   pl.BlockSpec((tm, LANES), lambda i: (i, 0)),
                   pl.BlockSpec((tm, LANES), lambda i: (i, 0))],
        out_shape=[jax.ShapeDtypeStruct((n_rows, D // 2), jnp.uint32),
                   jax.ShapeDtypeStruct((n_rows, LANES), jnp.int32),
                   jax.ShapeDtypeStruct((n_rows, LANES), F32)],
        compiler_params=_params(1),
    )(x, ada, g.reshape(1, D), wr, br.reshape(br.shape[0], 1, E))
    return h, top_i[:, :TOP_K], gates


def _moe_kernel(te_ref, nt_ref, tok_ref, h_ref, w1_ref, b1_ref, w2_ref, b2_ref, o_ref,
                x_buf, x_sem, w1_scr, w2_scr, *, n_chunks, n_tiles):
    t = pl.program_id(0)
    tm = o_ref.shape[0]
    ff = w2_ref.shape[0]
    fc = ff // n_chunks
    slot = t % 2

    def start_gather(tile, dst_slot):
        for r in range(tm):
            tok = tok_ref[tile * tm + r]
            pltpu.make_async_copy(h_ref.at[tok], x_buf.at[dst_slot, r], x_sem.at[dst_slot]).start()

    def wait_gather(dst_slot):
        pltpu.make_async_copy(h_ref.at[pl.ds(0, tm)], x_buf.at[dst_slot], x_sem.at[dst_slot]).wait()

    @pl.when(t == 0)
    def _():
        start_gather(0, 0)

    new_expert = jnp.logical_or(t == 0, te_ref[t] != te_ref[jnp.maximum(t - 1, 0)])

    used = t < nt_ref[0]

    @pl.when(jnp.logical_and(used, new_expert))
    def _():
        for c in range(2 * n_chunks):
            cols = slice(c * fc, (c + 1) * fc)
            w1_scr[:, cols] = w1_ref[:, cols].astype(BF16)
        for c in range(n_chunks):
            rows = slice(c * fc, (c + 1) * fc)
            w2_scr[rows, :] = w2_ref[rows, :].astype(BF16)

    wait_gather(slot)
    start_gather(jnp.minimum(t + 1, n_tiles - 1), 1 - slot)

    @pl.when(used)
    def _():
        words = jnp.concatenate([x_buf[slot, :, s, :] for s in range(x_buf.shape[2])], axis=-1)
        half = words.shape[1]
        x_lo = pltpu.bitcast(words << 16, F32).astype(BF16)
        x_hi = pltpu.bitcast(words & jnp.uint32(0xFFFF0000), F32).astype(BF16)

        def x_dot(w_cols):
            return (jnp.dot(x_lo, w1_scr[:half, w_cols], preferred_element_type=F32)
                    + jnp.dot(x_hi, w1_scr[half:, w_cols], preferred_element_type=F32))

        y = jnp.zeros(o_ref.shape, F32)
        for c in range(n_chunks):
            glu_cols = slice(c * fc, (c + 1) * fc)
            lin_cols = slice(ff + c * fc, ff + (c + 1) * fc)
            glu = x_dot(glu_cols) + b1_ref[:, glu_cols]
            lin = x_dot(lin_cols) + b1_ref[:, lin_cols]
            glu = jnp.minimum(glu, SWIGLU_LIMIT)
            lin = jnp.clip(lin, -SWIGLU_LIMIT, SWIGLU_LIMIT)
            act = glu * jax.nn.sigmoid(SWIGLU_ALPHA * glu) * (lin + 1.0)
            y = y + jnp.dot(act.astype(BF16), w2_scr[glu_cols, :], preferred_element_type=F32)
        o_ref[...] = (y + b2_ref[...]).astype(o_ref.dtype)

    @pl.when(jnp.logical_not(used))
    def _():
        o_ref[...] = jnp.zeros(o_ref.shape, o_ref.dtype)

    @pl.when(t == n_tiles - 1)
    def _():
        wait_gather(1 - slot)


def moe_experts(h, row_token, tile_expert, n_tiles_used, w1, b1, w2, b2, layer, *, tm):
    N = h.shape[0]
    D = w1.shape[2]
    P = row_token.shape[0]
    assert h.shape[1] * 2 == D and h.shape[1] % LANES == 0
    h = h.reshape(N, h.shape[1] // LANES, LANES)
    E, _, F2 = w1.shape[1:]
    ff = F2 // 2
    n_tiles = P // tm
    kernel = functools.partial(_moe_kernel, n_chunks=3, n_tiles=n_tiles)
    grid_spec = pltpu.PrefetchScalarGridSpec(
        num_scalar_prefetch=3,
        grid=(n_tiles,),
        in_specs=[pl.BlockSpec(memory_space=pl.ANY),
                  pl.BlockSpec((None, None, D, F2), lambda t, te, nt, tok: (layer, te[t], 0, 0)),
                  pl.BlockSpec((None, None, 1, F2), lambda t, te, nt, tok: (layer, te[t], 0, 0)),
                  pl.BlockSpec((None, None, ff, D), lambda t, te, nt, tok: (layer, te[t], 0, 0)),
                  pl.BlockSpec((None, None, 1, D), lambda t, te, nt, tok: (layer, te[t], 0, 0))],
        out_specs=pl.BlockSpec((tm, D), lambda t, te, nt, tok: (t, 0)),
        scratch_shapes=[pltpu.VMEM((2, tm) + h.shape[1:], jnp.uint32), pltpu.SemaphoreType.DMA((2,)),
                        pltpu.VMEM((D, F2), BF16), pltpu.VMEM((ff, D), BF16)],
    )
    return pl.pallas_call(
        kernel,
        grid_spec=grid_spec,
        out_shape=jax.ShapeDtypeStruct((P, D), BF16),
        compiler_params=_params(1),
    )(tile_expert, n_tiles_used, row_token, h, w1, b1.reshape(b1.shape[0], E, 1, F2), w2,
      b2.reshape(b2.shape[0], E, 1, D))


def moe_ffn(h, top_i, w1, b1, w2, b2, layer, *, tm=256):
    N = h.shape[0]
    E = w1.shape[1]
    P = N * TOP_K
    n_tiles = P // tm + E
    i32 = jnp.int32

    pair_e = top_i.reshape(P)
    pair_ids = jnp.arange(P, dtype=i32)
    sorted_e, order = lax.sort((pair_e, pair_ids), num_keys=1, is_stable=True)
    _, rank = lax.sort((order, pair_ids), num_keys=1)
    bounds = jnp.searchsorted(sorted_e, jnp.arange(E + 1, dtype=i32), side="left").astype(i32)
    start_sorted, counts = bounds[:E], bounds[1:] - bounds[:E]
    tiles_per = (counts + tm - 1) // tm
    tile_end = jnp.cumsum(tiles_per)
    tile_start = tile_end - tiles_per
    n_used = tile_end[-1:]
    tile_ids = jnp.minimum(jnp.arange(n_tiles, dtype=i32), n_used - 1)
    tile_expert = jnp.sum((tile_ids[:, None] >= tile_end[None, :]).astype(i32), axis=1)

    all_tiles = jnp.arange(n_tiles, dtype=i32)
    tile_off0 = (all_tiles - tile_start[tile_expert]) * tm
    off = tile_off0[:, None] + jnp.arange(tm, dtype=i32)[None, :]
    valid = (off < counts[tile_expert][:, None]) & (all_tiles < n_used)[:, None]
    sorted_pos = jnp.clip(start_sorted[tile_expert][:, None] + off, 0, P - 1)
    row_pair = order[sorted_pos.reshape(-1)]
    row_token = jnp.where(valid.reshape(-1), row_pair // TOP_K, 0)
    pair_row = (tile_start[pair_e] * tm + rank - start_sorted[pair_e]).reshape(N, TOP_K)

    ys = moe_experts(h, row_token, tile_expert, n_used, w1, b1, w2, b2, layer, tm=tm)
    return [ys[pair_row[:, k]] for k in range(TOP_K)]


def _combine_kernel(x_ref, gate_ref, tg_ref, *refs, stream):
    y_refs, o_ref = refs[:-1], refs[-1]
    tm = x_ref.shape[0]
    gate = gate_ref[pl.ds(stream.mod_row(pl.program_id(0) * tm), 1), :]
    tg = tg_ref[...]
    y = functools.reduce(jnp.add, [tg[:, k:k + 1] * y_ref[...].astype(F32) for k, y_ref in enumerate(y_refs)])
    o_ref[...] = x_ref[...] + gate * y


def combine_residual(x, ys, token_gates, ada, gate_col, stream, *, tm=512):
    n, D = ys[0].shape
    tm = stream.row_tile(tm)
    assert n % tm == 0
    row_spec = pl.BlockSpec((tm, D), lambda i: (i, 0))
    return pl.pallas_call(
        functools.partial(_combine_kernel, stream=stream),
        grid=(n // tm,),
        in_specs=[row_spec, pl.BlockSpec((ADA_ROWS, D), lambda i: (0, gate_col)),
                  pl.BlockSpec((tm, LANES), lambda i: (i, 0))] + [row_spec] * len(ys),
        out_specs=row_spec,
        out_shape=jax.ShapeDtypeStruct((n, D), F32),
        compiler_params=_params(1),
    )(x, ada, token_gates, *ys)


def rms_norm(x, g):
    return x * lax.rsqrt(jnp.mean(x * x, axis=-1, keepdims=True) + EPS) * g


def kernel(x, c, ctx, c_ctx, ada_w, ada_b, norm_g, final_g, ev_w_in, ev_w_out, hy_conv_w, hy_conv_b,
           hy_f_w1, hy_f_b1, hy_f_w2, hy_f_b2, hy_f_w3, hy_f_b3, hy_f_w4, hy_skip, gm_ln_g, gm_ws, gm_bs,
           od_w_qkv, od_w_out, od_lambda, od_subln_g, moe_wr, moe_br, moe_w1, moe_b1, moe_w2, moe_b2):
    B, L, D = x.shape
    n_ctx = ctx.shape[1]
    depth = ada_w.shape[0]
    hy_width = hy_skip.shape[2]
    hy_split = (HY_ORDER + 1) * hy_width
    gm_width = gm_ln_g.shape[1]
    stream = Stream(B, L, n_ctx)
    n_lat, M = stream.n_lat, stream.rows

    xs = jnp.concatenate([x.reshape(n_lat, D), ctx.reshape(B * n_ctx, D)], axis=0)
    cond = jnp.concatenate([jax.nn.silu(c), jax.nn.silu(c_ctx)[None], jnp.zeros((ADA_ROWS - B - 1, D), F32)], axis=0)
    ev_w_in, ev_w_out, od_w_qkv, od_w_out = (w.astype(BF16) for w in (ev_w_in, ev_w_out, od_w_qkv, od_w_out))

    for l in range(depth):
        i = l // 2
        last = l == depth - 1
        ada = matmul(cond.astype(BF16), ada_w, l, tm=ADA_ROWS, tn=1024) + ada_b[l]

        if l % 2 == 0:
            p = norm_matmul(xs, ada, norm_g[l, 0], ev_w_in, i, stream, shift_col=0, scale_col=1, out_dtype=BF16)
            filt = (hy_f_w1[i], hy_f_b1[i], hy_f_w2[i], hy_f_b2[i], hy_f_w3[i], hy_f_b3[i], hy_f_w4[i])
            y_a = jnp.concatenate(
                [hyena_mixer(p, 0, B, L, hy_width, hy_conv_w, hy_conv_b, filt, hy_skip, i),
                 hyena_mixer(p, n_lat, B, n_ctx, hy_width, hy_conv_w, hy_conv_b, filt, hy_skip, i)], axis=0)
            y_b = chunk_gmlp(p, hy_split, gm_width, gm_ln_g, gm_ws, gm_bs, i)
            xs = matmul_residual([y_a, y_b], ev_w_out, i, xs, ada, 2, stream)
        else:
            lam_init = 0.8 - 0.6 * math.exp(-0.3 * l)
            W = od_w_qkv.shape[2] // 3
            qkv = norm_matmul(xs, ada, norm_g[l, 0], od_w_qkv, i, stream, shift_col=0, scale_col=1, out_dtype=BF16,
                              rope_cols=(2 * W, W, math.log2(math.e) / math.sqrt(DA_HEAD_DIM)))
            lp = od_lambda[i]
            lam = (jnp.exp(jnp.sum(lp[0] * lp[1])) - jnp.exp(jnp.sum(lp[2] * lp[3])) + lam_init).reshape(1)
            o = diff_attention(qkv, lam, od_subln_g, i, 1.0 - lam_init, stream, ctx_queries=not last)
            xs = matmul_residual([o], od_w_out, i, xs, ada, 2, stream)

        n_tok = n_lat if last else M
        h, top_i, gates = route(xs, n_tok, ada, norm_g[l, 1], moe_wr, moe_br, l, stream, shift_col=3, scale_col=4)
        outs = moe_ffn(h, top_i, moe_w1, moe_b1, moe_w2, moe_b2, l)
        xs = combine_residual(xs, outs, gates, ada, 5, stream)

    return rms_norm(xs[:n_lat].reshape(B, L, D), final_g)
```
